```python
import math
import jax, jax.numpy as jnp
from jax import lax
import numpy as np

D_MODEL = 1024
BATCH = 8
SEQ = 4096
DEPTH = 2

ATTN_HEADS = 8
ATTN_HEAD_DIM = 64
ATTN_WIDTH = ATTN_HEADS * ATTN_HEAD_DIM
HGRN_HEADS = 4
HGRN_HEAD_DIM = 128
HGRN_WIDTH = HGRN_HEADS * HGRN_HEAD_DIM
MIX_WIDTH = ATTN_WIDTH + HGRN_WIDTH
IN_PROJ_WIDTH = 3 * ATTN_WIDTH + 4 * HGRN_WIDTH
DILATED_PATTERNS = ((128, 1), (512, 4), (2048, 16))
ROPE_THETA = 10000.0
HGRN_CHUNK = 16
MLP_HIDDEN = 4 * D_MODEL
NORM_EPS = 1e-6
MASK_VALUE = -1e30

kernel_name = 'hymba_hgrn2_dilated_swa_hybrid'


def rms_norm(x, gain):
    xf = x.astype(jnp.float32)
    xf = xf * lax.rsqrt(jnp.mean(xf * xf, axis=-1, keepdims=True) + NORM_EPS)
    return (xf * gain.astype(jnp.float32)).astype(x.dtype)


def split_heads(a, n_heads, head_dim):
    b, s, _ = a.shape
    return a.reshape(b, s, n_heads, head_dim).transpose(0, 2, 1, 3)


def merge_heads(a):
    b, h, s, d = a.shape
    return a.transpose(0, 2, 1, 3).reshape(b, s, h * d)


def rotary(x, positions):
    half = x.shape[-1] // 2
    inv_freq = ROPE_THETA ** (-jnp.arange(half, dtype=jnp.float32) / half)
    ang = positions.astype(jnp.float32)[:, None] * inv_freq[None, :]
    cos, sin = jnp.cos(ang), jnp.sin(ang)
    x1, x2 = x[..., :half], x[..., half:]
    return jnp.concatenate([x1 * cos - x2 * sin, x1 * sin + x2 * cos], axis=-1)


def dilated_window_attention(q, k, v, window, dilation):
    b, h, s, d = q.shape
    span = window // dilation
    unit = dilation * span
    s_pad = -(-s // unit) * unit
    pad = ((0, 0), (0, 0), (0, s_pad - s), (0, 0))
    q, k, v = jnp.pad(q, pad), jnp.pad(k, pad), jnp.pad(v, pad)
    n_sub = s_pad // dilation
    n_blk = n_sub // span

    def to_sub(a):
        a = a.reshape(b, h, n_sub, dilation, d).transpose(0, 1, 3, 2, 4)
        return a.reshape(b, h, dilation, n_blk, span, d)

    def with_prev(a):
        prev = jnp.pad(a, ((0, 0), (0, 0), (0, 0), (1, 0), (0, 0), (0, 0)))[:, :, :, :-1]
        return jnp.concatenate([prev, a], axis=4)

    qs = to_sub(q)
    kc, vc = with_prev(to_sub(k)), with_prev(to_sub(v))
    scores = jnp.einsum('bhrnid,bhrnjd->bhrnij', qs, kc)
    i = jnp.arange(span)[:, None]
    j = jnp.arange(2 * span)[None, :]
    dist = span + i - j
    band = (dist >= 0) & (dist <= span)
    blk = jnp.arange(n_blk)[:, None, None]
    valid = band[None] & ((blk > 0) | (j >= span)[None])
    scores = jnp.where(valid, scores, MASK_VALUE)
    m = jnp.max(scores, axis=-1, keepdims=True)
    p = jnp.where(valid, jnp.exp(scores - m), 0.0)
    l = jnp.sum(p, axis=-1, keepdims=True)
    o = jnp.einsum('bhrnij,bhrnjd->bhrnid', p, vc) / l
    lse = (m + jnp.log(l))[..., 0]

    def from_sub(a):
        tail = a.shape[5:]
        a = a.reshape((b, h, dilation, n_sub) + tail)
        a = jnp.moveaxis(a, 2, 3)
        return a.reshape((b, h, s_pad) + tail)[:, :, :s]

    return from_sub(o), from_sub(lse)


def dilated_attention_group(q_a, k_a, v_a, positions):
    q = rotary(split_heads(q_a, ATTN_HEADS, ATTN_HEAD_DIM).astype(jnp.float32), positions)
    q = q * (ATTN_HEAD_DIM ** -0.5)
    k = rotary(split_heads(k_a, ATTN_HEADS, ATTN_HEAD_DIM).astype(jnp.float32), positions)
    v = split_heads(v_a, ATTN_HEADS, ATTN_HEAD_DIM).astype(jnp.float32)
    outs, lses = [], []
    for window, dilation in DILATED_PATTERNS:
        o, lse = dilated_window_attention(q, k, v, window, dilation)
        outs.append(o)
        lses.append(lse)
    weights = jax.nn.softmax(jnp.stack(lses, axis=0), axis=0)
    o = jnp.einsum('pbhs,pbhsd->bhsd', weights, jnp.stack(outs, axis=0))
    return merge_heads(o)


def hgrn_lower_bounds(lb_logits):
    p = jax.nn.softmax(lb_logits.astype(jnp.float32), axis=0)
    return jnp.cumsum(p, axis=0) - p[0]


def hgrn2_chunkwise(q, k, v, log_f):
    b, h, s, kd = q.shape
    vd = v.shape[-1]
    c = HGRN_CHUNK
    n = s // c
    q = q.reshape(b, h, n, c, kd)
    k = k.reshape(b, h, n, c, kd)
    v = v.reshape(b, h, n, c, vd)
    g = jnp.cumsum(log_f.reshape(b, h, n, c, kd), axis=3)
    g_last = g[:, :, :, -1:]
    causal = jnp.tril(jnp.ones((c, c), dtype=bool))[:, :, None]
    diff = g[:, :, :, :, None, :] - g[:, :, :, None, :, :]
    decay = jnp.where(causal, jnp.exp(jnp.where(causal, diff, 0.0)), 0.0)
    a = jnp.einsum('bhnik,bhnjk,bhnijk->bhnij', q, k, decay)
    o_intra = jnp.einsum('bhnij,bhnjv->bhniv', a, v)
    q_in = q * jnp.exp(g)
    k_out = k * jnp.exp(g_last - g)
    chunk_decay = jnp.exp(g_last[:, :, :, 0])

    def step(state, xs):
        q_n, k_n, v_n, dec_n = xs
        o_n = jnp.einsum('bhik,bhkv->bhiv', q_n, state)
        state = dec_n[..., None] * state + jnp.einsum('bhjk,bhjv->bhkv', k_n, v_n)
        return state, o_n

    xs = (jnp.moveaxis(q_in, 2, 0), jnp.moveaxis(k_out, 2, 0),
          jnp.moveaxis(v, 2, 0), jnp.moveaxis(chunk_decay, 2, 0))
    state0 = jnp.zeros((b, h, kd, vd), dtype=jnp.float32)
    _, o_inter = lax.scan(step, state0, xs)
    o = o_intra + jnp.moveaxis(o_inter, 0, 2)
    return o.reshape(b, h, s, vd)


def hgrn2_group(q_h, f_h, i_h, g_h, lower_bound, out_gain):
    q = jax.nn.silu(split_heads(q_h, HGRN_HEADS, HGRN_HEAD_DIM).astype(jnp.float32))
    q = q * (HGRN_HEAD_DIM ** -0.5)
    z = split_heads(f_h, HGRN_HEADS, HGRN_HEAD_DIM).astype(jnp.float32)
    v = split_heads(i_h, HGRN_HEADS, HGRN_HEAD_DIM).astype(jnp.float32)
    lb = lower_bound.reshape(HGRN_HEADS, 1, HGRN_HEAD_DIM)
    log_f = jnp.log(lb + (1.0 - lb) * jax.nn.sigmoid(z))
    k = (1.0 - lb) * jax.nn.sigmoid(-z)
    o = hgrn2_chunkwise(q, k, v, log_f)
    o = rms_norm(o, out_gain)
    gate = jax.nn.silu(split_heads(g_h, HGRN_HEADS, HGRN_HEAD_DIM).astype(jnp.float32))
    return merge_heads(o * gate)


def setup_inputs(seed: int = 0) -> dict:
    key = jax.random.key(seed)
    ks = jax.random.split(key, 12)
    f32 = jnp.float32
    x = jax.random.normal(ks[0], (BATCH, SEQ, D_MODEL), f32)
    norm_mix = 1.0 + 0.02 * jax.random.normal(ks[1], (DEPTH, D_MODEL), f32)
    w_in = jax.random.normal(ks[2], (DEPTH, D_MODEL, IN_PROJ_WIDTH), f32) * D_MODEL ** -0.5
    attn_out_gain = 1.0 + 0.02 * jax.random.normal(ks[3], (DEPTH, ATTN_WIDTH), f32)
    hgrn_lb_logits = 0.1 * jax.random.normal(ks[4], (DEPTH, HGRN_WIDTH), f32)
    hgrn_out_gain = 1.0 + 0.02 * jax.random.normal(ks[5], (DEPTH, HGRN_HEAD_DIM), f32)
    w_out = jax.random.normal(ks[6], (DEPTH, MIX_WIDTH, D_MODEL), f32) * MIX_WIDTH ** -0.5
    norm_mlp = 1.0 + 0.02 * jax.random.normal(ks[7], (DEPTH, D_MODEL), f32)
    w_up = jax.random.normal(ks[8], (DEPTH, D_MODEL, MLP_HIDDEN), f32) * D_MODEL ** -0.5
    w_down = jax.random.normal(ks[9], (DEPTH, MLP_HIDDEN, D_MODEL), f32) * MLP_HIDDEN ** -0.5
    norm_final = 1.0 + 0.02 * jax.random.normal(ks[10], (D_MODEL,), f32)
    return {'x': x, 'norm_mix': norm_mix, 'w_in': w_in, 'attn_out_gain': attn_out_gain,
            'hgrn_lb_logits': hgrn_lb_logits, 'hgrn_out_gain': hgrn_out_gain,
            'w_out': w_out, 'norm_mlp': norm_mlp, 'w_up': w_up, 'w_down': w_down,
            'norm_final': norm_final}


def reference(x, norm_mix, w_in, attn_out_gain, hgrn_lb_logits, hgrn_out_gain,
              w_out, norm_mlp, w_up, w_down, norm_final):
    seq = x.shape[1]
    positions = jnp.arange(seq, dtype=jnp.int32)
    lower_bounds = hgrn_lower_bounds(hgrn_lb_logits)
    split_points = [ATTN_WIDTH, 2 * ATTN_WIDTH, 3 * ATTN_WIDTH,
                    3 * ATTN_WIDTH + HGRN_WIDTH, 3 * ATTN_WIDTH + 2 * HGRN_WIDTH,
                    3 * ATTN_WIDTH + 3 * HGRN_WIDTH]
    for layer in range(DEPTH):
        h = rms_norm(x, norm_mix[layer])
        proj = h @ w_in[layer]
        q_a, k_a, v_a, q_h, f_h, i_h, g_h = jnp.split(proj, split_points, axis=-1)
        attn = dilated_attention_group(q_a, k_a, v_a, positions)
        attn = rms_norm(attn, attn_out_gain[layer])
        rec = hgrn2_group(q_h, f_h, i_h, g_h, lower_bounds[layer], hgrn_out_gain[layer])
        mixed = jnp.concatenate([attn, rec], axis=-1).astype(x.dtype)
        x = x + mixed @ w_out[layer]
        h = rms_norm(x, norm_mlp[layer])
        x = x + jnp.square(jax.nn.relu(h @ w_up[layer])) @ w_down[layer]
    return rms_norm(x, norm_final)
```

```python
import functools
import math

import jax
import jax.numpy as jnp
from jax import lax
from jax.experimental import pallas as pl
from jax.experimental.pallas import tpu as pltpu

F32 = jnp.float32
BF16 = jnp.bfloat16

D_MODEL = 1024
ATTN_HEADS = 8
ATTN_HEAD_DIM = 64
ATTN_WIDTH = ATTN_HEADS * ATTN_HEAD_DIM
HGRN_HEADS = 4
HGRN_HEAD_DIM = 128
HGRN_WIDTH = HGRN_HEADS * HGRN_HEAD_DIM
MIX_WIDTH = ATTN_WIDTH + HGRN_WIDTH
IN_PROJ_WIDTH = 3 * ATTN_WIDTH + 4 * HGRN_WIDTH
DILATIONS = (1, 4, 16)
SPAN = 128
ROPE_THETA = 10000.0
MLP_HIDDEN = 4 * D_MODEL
NORM_EPS = 1e-6
MASK_VALUE = -1e30

LANES = 128
V7X_VMEM_LIMIT = 56 * 1024 * 1024
ROW_TILE = 512
COL_CHUNK = 512
CHUNK = 128
HGRN_STEP = 512
KEY_PAD = max(DILATIONS) * SPAN


def _rms(x, gain):
    return x * lax.rsqrt(jnp.mean(x * x, axis=-1, keepdims=True) + NORM_EPS) * gain


def _norm_in_proj_kernel(x_ref, g_ref, w_ref, o_ref):
    h = _rms(x_ref[...], g_ref[...]).astype(BF16)
    n_out = o_ref.shape[1]
    for c0 in range(0, n_out, COL_CHUNK):
        o_ref[:, c0:c0 + COL_CHUNK] = jnp.dot(
            h, w_ref[:, c0:c0 + COL_CHUNK], preferred_element_type=F32)


def _norm_in_proj(x2d, gain, w_bf16):
    t, d = x2d.shape
    n = w_bf16.shape[1]
    return pl.pallas_call(
        _norm_in_proj_kernel,
        grid=(t // ROW_TILE,),
        in_specs=[
            pl.BlockSpec((ROW_TILE, d), lambda i: (i, 0)),
            pl.BlockSpec((1, d), lambda i: (0, 0)),
            pl.BlockSpec((d, n), lambda i: (0, 0)),
        ],
        out_specs=pl.BlockSpec((ROW_TILE, n), lambda i: (i, 0)),
        out_shape=jax.ShapeDtypeStruct((t, n), F32),
        compiler_params=pltpu.CompilerParams(
            dimension_semantics=("arbitrary",), vmem_limit_bytes=V7X_VMEM_LIMIT),
        name="norm_in_proj",
    )(x2d, gain.reshape(1, d), w_bf16)


def _swap_half_heads(x):
    lane = lax.broadcasted_iota(jnp.int32, x.shape, 1)
    return jnp.where((lane & 32) == 0,
                     pltpu.roll(x, LANES - 32, axis=1), pltpu.roll(x, 32, axis=1))


def _attn_kernel(q_ref, k_ref, v_ref, cos_ref, sin_ref, o_ref,
                 qr, kr, vr, acc, m0, m1, l0, l1, *, seq):
    def prologue(t, carry):
        rows = pl.ds(pl.multiple_of(t * SPAN, SPAN), SPAN)
        pad_rows = pl.ds(pl.multiple_of(KEY_PAD + t * SPAN, SPAN), SPAN)
        cos = cos_ref[rows, :]
        sin = sin_ref[rows, :]
        q = q_ref[rows, :]
        k = k_ref[rows, :]
        qr[rows, :] = (q * cos + _swap_half_heads(q) * sin) * (ATTN_HEAD_DIM ** -0.5)
        kr[pad_rows, :] = k * cos + _swap_half_heads(k) * sin
        vr[pad_rows, :] = v_ref[rows, :]
        zero = jnp.zeros((SPAN, LANES), F32)
        acc[rows, :] = zero
        l0[rows, :] = zero
        l1[rows, :] = zero
        m0[rows, :] = jnp.full((SPAN, LANES), MASK_VALUE, F32)
        m1[rows, :] = jnp.full((SPAN, LANES), MASK_VALUE, F32)
        return carry

    lax.fori_loop(0, seq // SPAN, prologue, 0)
    kr[0:KEY_PAD, :] = jnp.zeros((KEY_PAD, LANES), F32)
    vr[0:KEY_PAD, :] = jnp.zeros((KEY_PAD, LANES), F32)

    head0 = lax.broadcasted_iota(jnp.int32, (SPAN, LANES), 1) < ATTN_HEAD_DIM
    ii = lax.broadcasted_iota(jnp.int32, (SPAN, 2 * SPAN), 0)
    jj = lax.broadcasted_iota(jnp.int32, (SPAN, 2 * SPAN), 1)
    ones = jnp.ones((2 * SPAN, LANES), BF16)
    n_tiles = seq // SPAN

    for dil in DILATIONS:
        n_blk = n_tiles // dil

        def rows(start, size, dil=dil):
            return pl.ds(start, size) if dil == 1 else pl.ds(start, size, stride=dil)

        def tile(t, carry, dil=dil, n_blk=n_blk, rows=rows):
            r = t // n_blk
            n = t % n_blk
            q_rows = rows(r + dil * SPAN * n, SPAN)
            k_rows = rows(KEY_PAD + r + dil * SPAN * (n - 1), 2 * SPAN)
            qt = qr[q_rows, :]
            kt = kr[k_rows, :].astype(BF16)
            vt = jnp.concatenate([vr[k_rows, :].astype(BF16), ones], axis=1)
            first_key = jnp.maximum(ii, jnp.where(n > 0, 0, SPAN))
            valid = (jj >= first_key) & (jj <= ii + SPAN)
            pv = []
            for head, (m_ref, l_ref) in enumerate(((m0, l0), (m1, l1))):
                qh = jnp.where(head0 if head == 0 else ~head0, qt, 0.0).astype(BF16)
                s = lax.dot_general(qh, kt, (((1,), (1,)), ((), ())),
                                    preferred_element_type=F32)
                s = jnp.where(valid, s, MASK_VALUE)
                m_old = m_ref[q_rows, :]
                m_new = jnp.maximum(m_old, jnp.max(s, axis=-1, keepdims=True))
                alpha = jnp.exp(m_old - m_new)
                p = jnp.exp(s - jnp.concatenate([m_new, m_new], axis=1))
                res = jnp.dot(p.astype(BF16), vt, preferred_element_type=F32)
                m_ref[q_rows, :] = m_new
                l_ref[q_rows, :] = alpha * l_ref[q_rows, :] + res[:, LANES:]
                pv.append((alpha, res[:, :LANES]))
            alpha = jnp.where(head0, pv[0][0], pv[1][0])
            acc[q_rows, :] = alpha * acc[q_rows, :] + jnp.where(head0, pv[0][1], pv[1][1])
            return carry

        lax.fori_loop(0, n_tiles, tile, 0)

    def epilogue(t, carry):
        rows = pl.ds(pl.multiple_of(t * SPAN, SPAN), SPAN)
        o_ref[rows, :] = acc[rows, :] / jnp.where(head0, l0[rows, :], l1[rows, :])
        return carry

    lax.fori_loop(0, seq // SPAN, epilogue, 0)


def _dilated_attention(proj3d, cos, sin):
    b, seq, _ = proj3d.shape
    n_pairs = ATTN_WIDTH // LANES
    blk = lambda off: pl.BlockSpec((None, seq, LANES), lambda i, j: (i, 0, off + j))
    tab = pl.BlockSpec((seq, LANES), lambda i, j: (0, 0))
    slab = pltpu.VMEM((seq, LANES), F32)
    padded = pltpu.VMEM((KEY_PAD + seq, LANES), F32)
    return pl.pallas_call(
        functools.partial(_attn_kernel, seq=seq),
        grid=(b, n_pairs),
        in_specs=[blk(0), blk(n_pairs), blk(2 * n_pairs), tab, tab],
        out_specs=pl.BlockSpec((None, seq, LANES), lambda i, j: (i, 0, j)),
        out_shape=jax.ShapeDtypeStruct((b, seq, ATTN_WIDTH), F32),
        scratch_shapes=[slab, padded, padded, slab, slab, slab, slab, slab],
        compiler_params=pltpu.CompilerParams(
            dimension_semantics=("arbitrary", "arbitrary"),
            vmem_limit_bytes=V7X_VMEM_LIMIT),
        name="dilated_attn",
    )(proj3d, proj3d, proj3d, cos, sin)


def _level_anchor(g, half):
    rows = g.shape[0]
    if 2 * half >= 8:
        g3 = g.reshape(rows // (2 * half), 2 * half, LANES)
        return jnp.broadcast_to(g3[:, half - 1:half, :], g3.shape).reshape(rows, LANES)
    g3 = g.reshape(rows // 8, 8, LANES)
    sub = lax.broadcasted_iota(jnp.int32, g3.shape, 1)
    out = jnp.broadcast_to(g3[:, half - 1:half, :], g3.shape)
    for first in range(2 * half, 8, 2 * half):
        src = jnp.broadcast_to(g3[:, first + half - 1:first + half, :], g3.shape)
        out = jnp.where(sub >= first, src, out)
    return out.reshape(rows, LANES)


def _chunk_cumsum(x):
    row = lax.broadcasted_iota(jnp.int32, x.shape, 0)
    shift = 1
    while shift < x.shape[0]:
        x = x + jnp.where(row >= shift, pltpu.roll(x, shift, axis=0), 0.0)
        shift *= 2
    return x


def _hgrn_kernel(q_ref, f_ref, i_ref, g_ref, lbl_ref, gain_ref, lvl_ref, o_ref, state,
                 *, layer):
    @pl.when(pl.program_id(2) == 0)
    def _():
        state[...] = jnp.zeros_like(state)

    logits = lbl_ref[...]
    e = jnp.exp(logits - jnp.max(logits, axis=0, keepdims=True))
    share = e / jnp.sum(e, axis=0, keepdims=True)
    lb = jnp.sum(share[0:layer + 1, :], axis=0, keepdims=True) - share[0:1, :]

    level = lvl_ref[...]
    n_levels = int(math.log2(CHUNK))
    odd_row = (lax.broadcasted_iota(jnp.int32, (CHUNK, LANES), 0) & 1) == 1
    for c0 in range(0, q_ref.shape[0], CHUNK):
        sl = slice(c0, c0 + CHUNK)
        qx = q_ref[sl, :]
        q = qx * jax.nn.sigmoid(qx) * (HGRN_HEAD_DIM ** -0.5)
        z = f_ref[sl, :]
        v = i_ref[sl, :].astype(BF16)
        t = jnp.exp(-jnp.abs(z))
        r = 1.0 / (1.0 + t)
        sig_pos = jnp.where(z >= 0, r, t * r)
        sig_neg = jnp.where(z >= 0, t * r, r)
        f = lb + (1.0 - lb) * sig_pos
        k = (1.0 - lb) * sig_neg
        g = _chunk_cumsum(jnp.log(f))
        g_last = g[CHUNK - 1:CHUNK, :]

        a = lax.dot_general(q.astype(BF16), k.astype(BF16), (((1,), (1,)), ((), ())),
                            preferred_element_type=F32)
        a = jnp.where(level == n_levels, a, 0.0)
        for lv in range(n_levels):
            half = 1 << lv
            if lv == 0:
                x = jnp.where(odd_row, f, 1.0)
            else:
                x = jnp.exp(-jnp.abs(g - _level_anchor(g, half)))
            a_lv = lax.dot_general((q * x).astype(BF16), (k * x).astype(BF16),
                                   (((1,), (1,)), ((), ())), preferred_element_type=F32)
            a = jnp.where(level == lv, a_lv, a)
        o = jnp.dot(a.astype(BF16), v, preferred_element_type=F32)

        st = state[...]
        q_in = (q * jnp.exp(g)).astype(BF16)
        o = o + lax.dot_general(q_in, st.astype(BF16), (((1,), (1,)), ((), ())),
                                preferred_element_type=F32)
        k_out = (k * jnp.exp(g_last - g)).astype(BF16)
        state[...] = st * jnp.exp(g_last) + lax.dot_general(
            v, k_out, (((0,), (0,)), ((), ())), preferred_element_type=F32)

        gx = g_ref[sl, :]
        o_ref[sl, :] = _rms(o, gain_ref[...]) * (gx * jax.nn.sigmoid(gx))


def _hgrn2(proj3d, lb_logits, out_gain, level_map, layer):
    b, seq, _ = proj3d.shape
    base = 3 * ATTN_WIDTH // LANES
    blk = lambda off: pl.BlockSpec((None, HGRN_STEP, LANES),
                                   lambda i, h, s: (i, s, base + off * HGRN_HEADS + h))
    depth = lb_logits.shape[0]
    return pl.pallas_call(
        functools.partial(_hgrn_kernel, layer=layer),
        grid=(b, HGRN_HEADS, seq // HGRN_STEP),
        in_specs=[
            blk(0), blk(1), blk(2), blk(3),
            pl.BlockSpec((depth, LANES), lambda i, h, s: (0, h)),
            pl.BlockSpec((1, LANES), lambda i, h, s: (0, 0)),
            pl.BlockSpec((CHUNK, CHUNK), lambda i, h, s: (0, 0)),
        ],
        out_specs=pl.BlockSpec((None, HGRN_STEP, LANES), lambda i, h, s: (i, s, h)),
        out_shape=jax.ShapeDtypeStruct((b, seq, HGRN_WIDTH), F32),
        scratch_shapes=[pltpu.VMEM((HGRN_HEAD_DIM, HGRN_HEAD_DIM), F32)],
        compiler_params=pltpu.CompilerParams(
            dimension_semantics=("arbitrary", "arbitrary", "arbitrary"),
            vmem_limit_bytes=V7X_VMEM_LIMIT),
        name="hgrn2",
    )(proj3d, proj3d, proj3d, proj3d, lb_logits, out_gain.reshape(1, LANES), level_map)


def _intra_chunk_levels():
    i = jnp.arange(CHUNK, dtype=jnp.int32)[:, None]
    j = jnp.arange(CHUNK, dtype=jnp.int32)[None, :]
    x = i ^ j
    lv = jnp.zeros((CHUNK, CHUNK), jnp.int32)
    for bit in range(1, int(math.log2(CHUNK))):
        lv = jnp.where(x >= (1 << bit), bit, lv)
    lv = jnp.where(i == j, int(math.log2(CHUNK)), lv)
    return jnp.where(j > i, -1, lv)


def _mix_mlp_kernel(x_ref, attn_ref, rec_ref, ag_ref, wo_ref, nm_ref, wu_ref, wd_ref,
                    nf_ref, o_ref, *, final_norm):
    attn = _rms(attn_ref[...], ag_ref[...]).astype(BF16)
    x1 = (x_ref[...]
          + jnp.dot(attn, wo_ref[0:ATTN_WIDTH, :], preferred_element_type=F32)
          + jnp.dot(rec_ref[...].astype(BF16), wo_ref[ATTN_WIDTH:MIX_WIDTH, :],
                    preferred_element_type=F32))
    h = _rms(x1, nm_ref[...]).astype(BF16)
    o_ref[...] = x1
    for c0 in range(0, MLP_HIDDEN, COL_CHUNK):
        u = jnp.dot(h, wu_ref[:, c0:c0 + COL_CHUNK], preferred_element_type=F32)
        u = jnp.square(jnp.maximum(u, 0.0)).astype(BF16)
        o_ref[...] += jnp.dot(u, wd_ref[c0:c0 + COL_CHUNK, :], preferred_element_type=F32)
    if final_norm:
        o_ref[...] = _rms(o_ref[...], nf_ref[...])


def _mix_mlp(x2d, attn2d, rec2d, attn_gain, w_out, norm_mlp, w_up, w_down, norm_final,
             final_norm):
    t, d = x2d.shape
    row = lambda w: pl.BlockSpec((ROW_TILE, w), lambda i: (i, 0))
    full = lambda a: pl.BlockSpec(a.shape, lambda i: (0, 0))
    vec = lambda a: a.reshape(1, -1)
    args = (x2d, attn2d, rec2d, vec(attn_gain), w_out, vec(norm_mlp), w_up, w_down,
            vec(norm_final))
    return pl.pallas_call(
        functools.partial(_mix_mlp_kernel, final_norm=final_norm),
        grid=(t // ROW_TILE,),
        in_specs=[row(d), row(ATTN_WIDTH), row(HGRN_WIDTH)] + [full(a) for a in args[3:]],
        out_specs=row(d),
        out_shape=jax.ShapeDtypeStruct((t, d), F32),
        compiler_params=pltpu.CompilerParams(
            dimension_semantics=("arbitrary",), vmem_limit_bytes=V7X_VMEM_LIMIT),
        name="mix_mlp",
    )(*args)


def _rotary_tables(seq):
    half = ATTN_HEAD_DIM // 2
    inv_freq = ROPE_THETA ** (-jnp.arange(half, dtype=F32) / half)
    ang = jnp.arange(seq, dtype=F32)[:, None] * inv_freq[None, :]
    cos, sin = jnp.cos(ang), jnp.sin(ang)
    reps = LANES // ATTN_HEAD_DIM
    return (jnp.tile(jnp.concatenate([cos, cos], axis=1), (1, reps)),
            jnp.tile(jnp.concatenate([-sin, sin], axis=1), (1, reps)))


def kernel(x, norm_mix, w_in, attn_out_gain, hgrn_lb_logits, hgrn_out_gain, w_out, norm_mlp,
           w_up, w_down, norm_final):
    b, seq, d = x.shape
    depth = w_in.shape[0]
    cos, sin = _rotary_tables(seq)
    level_map = _intra_chunk_levels()
    x2d = x.reshape(b * seq, d)
    for layer in range(depth):
        proj = _norm_in_proj(x2d, norm_mix[layer], w_in[layer].astype(BF16))
        proj3d = proj.reshape(b, seq, IN_PROJ_WIDTH)
        attn = _dilated_attention(proj3d, cos, sin)
        rec = _hgrn2(proj3d, hgrn_lb_logits, hgrn_out_gain[layer], level_map, layer)
        x2d = _mix_mlp(x2d, attn.reshape(b * seq, ATTN_WIDTH), rec.reshape(b * seq, HGRN_WIDTH),
                       attn_out_gain[layer], w_out[layer].astype(BF16), norm_mlp[layer],
                       w_up[layer].astype(BF16), w_down[layer].astype(BF16), norm_final,
                       final_norm=(layer == depth - 1))
    return x2d.reshape(b, seq, d)
```

```python
import functools
import math

import jax
import jax.numpy as jnp
from jax import lax
from jax.experimental import pallas as pl
from jax.experimental.pallas import tpu as pltpu

F32 = jnp.float32
BF16 = jnp.bfloat16

D_MODEL = 1024
ATTN_HEADS = 8
ATTN_HEAD_DIM = 64
ATTN_WIDTH = ATTN_HEADS * ATTN_HEAD_DIM
HGRN_HEADS = 4
HGRN_HEAD_DIM = 128
HGRN_WIDTH = HGRN_HEADS * HGRN_HEAD_DIM
MIX_WIDTH = ATTN_WIDTH + HGRN_WIDTH
IN_PROJ_WIDTH = 3 * ATTN_WIDTH + 4 * HGRN_WIDTH
DILATIONS = (1, 4, 16)
SPAN = 128
ROPE_THETA = 10000.0
MLP_HIDDEN = 4 * D_MODEL
NORM_EPS = 1e-6
MASK_VALUE = -1e30

LANES = 128
V7X_VMEM_LIMIT = 56 * 1024 * 1024
ROW_TILE = 512
COL_CHUNK = 512
CHUNK = 128
HGRN_STEP = 512
KEY_PAD = max(DILATIONS) * SPAN
ATTN_UNROLL = 4
ROPE_UNROLL = 4
LOG2_E = 1.4426950408889634


def _rms(x, gain):
    return x * lax.rsqrt(jnp.mean(x * x, axis=-1, keepdims=True) + NORM_EPS) * gain


def _norm_in_proj_kernel(x_ref, g_ref, w_ref, o_ref):
    h = _rms(x_ref[...], g_ref[...]).astype(BF16)
    n_out = o_ref.shape[1]
    for c0 in range(0, n_out, COL_CHUNK):
        o_ref[:, c0:c0 + COL_CHUNK] = jnp.dot(
            h, w_ref[:, c0:c0 + COL_CHUNK], preferred_element_type=F32)


def _norm_in_proj(x2d, gain, w_bf16):
    t, d = x2d.shape
    n = w_bf16.shape[1]
    return pl.pallas_call(
        _norm_in_proj_kernel,
        grid=(t // ROW_TILE,),
        in_specs=[
            pl.BlockSpec((ROW_TILE, d), lambda i: (i, 0)),
            pl.BlockSpec((1, d), lambda i: (0, 0)),
            pl.BlockSpec((d, n), lambda i: (0, 0)),
        ],
        out_specs=pl.BlockSpec((ROW_TILE, n), lambda i: (i, 0)),
        out_shape=jax.ShapeDtypeStruct((t, n), F32),
        compiler_params=pltpu.CompilerParams(
            dimension_semantics=("arbitrary",), vmem_limit_bytes=V7X_VMEM_LIMIT),
        name="norm_in_proj",
    )(x2d, gain.reshape(1, d), w_bf16)


def _attn_kernel(q_ref, k_ref, v_ref, cos_ref, sin_ref, o_ref,
                 qr, kr, vr, out0, out1, out2, lse0, lse1, lse2, p_buf, m_buf, *, seq):
    def prologue(t, carry):
        for u in range(ROPE_UNROLL):
            start = (t * ROPE_UNROLL + u) * SPAN
            rows = pl.ds(pl.multiple_of(start, SPAN), SPAN)
            pad_rows = pl.ds(pl.multiple_of(KEY_PAD + start, SPAN), SPAN)
            cos = cos_ref[rows, :]
            sin = sin_ref[rows, :]
            q = q_ref[rows, :]
            k = k_ref[rows, :]
            qr[rows, :] = ((q * cos + pltpu.roll(q, LANES // 2, axis=1) * sin)
                           * (ATTN_HEAD_DIM ** -0.5 * LOG2_E))
            kr[pad_rows, :] = k * cos + pltpu.roll(k, LANES // 2, axis=1) * sin
            vr[pad_rows, :] = v_ref[rows, :]
        return carry

    lax.fori_loop(0, seq // (SPAN * ROPE_UNROLL), prologue, 0)
    kr[0:KEY_PAD, :] = jnp.zeros((KEY_PAD, LANES), F32)
    vr[0:KEY_PAD, :] = jnp.zeros((KEY_PAD, LANES), F32)

    lane = lax.broadcasted_iota(jnp.int32, (SPAN, LANES), 1)
    head0 = lane < ATTN_HEAD_DIM
    qk_head0 = (lane & (ATTN_HEAD_DIM // 2)) == 0
    q_minus_k = (lax.broadcasted_iota(jnp.int32, (SPAN, SPAN), 0)
                 - lax.broadcasted_iota(jnp.int32, (SPAN, SPAN), 1))
    ones = jnp.ones((2 * SPAN, LANES), BF16)
    n_tiles = seq // SPAN

    for dil, out_p, lse_p in zip(DILATIONS, (out0, out1, out2), (lse0, lse1, lse2)):
        n_blk = n_tiles // dil

        def rows(start, size, dil=dil):
            return pl.ds(start, size) if dil == 1 else pl.ds(start, size, stride=dil)

        def tile_rows(t, dil=dil, n_blk=n_blk, rows=rows):
            r = t // n_blk
            n = t % n_blk
            return (n, rows(r + dil * SPAN * n, SPAN),
                    rows(KEY_PAD + r + dil * SPAN * (n - 1), 2 * SPAN))

        def probs(t, slot, tile_rows=tile_rows):
            n, q_rows, k_rows = tile_rows(t)
            qt = qr[q_rows, :]
            kt = kr[k_rows, :].astype(BF16)
            prev_ok = q_minus_k <= jnp.where(n > 0, 0, -SPAN - 1)
            cur_ok = q_minus_k >= 0
            scores = []
            for head in range(2):
                qh = jnp.where(qk_head0 if head == 0 else ~qk_head0, qt, 0.0).astype(BF16)
                scores.append(lax.dot_general(qh, kt, (((1,), (1,)), ((), ())),
                                              preferred_element_type=F32))
            mx = []
            for head, s in enumerate(scores):
                s_prev = jnp.where(prev_ok, s[:, :SPAN], MASK_VALUE)
                s_cur = jnp.where(cur_ok, s[:, SPAN:], MASK_VALUE)
                m = jnp.max(jnp.maximum(s_prev, s_cur), axis=-1, keepdims=True)
                p_buf[slot, head, :, 0:SPAN] = jnp.exp2(s_prev - m).astype(BF16)
                p_buf[slot, head, :, SPAN:2 * SPAN] = jnp.exp2(s_cur - m).astype(BF16)
                mx.append(jnp.broadcast_to(m, (SPAN, LANES)))
            m_buf[slot] = jnp.where(head0, mx[0], mx[1])

        def weighted_values(t, slot, tile_rows=tile_rows, out_p=out_p, lse_p=lse_p):
            _, q_rows, k_rows = tile_rows(t)
            vt = jnp.concatenate([vr[k_rows, :].astype(BF16), ones], axis=1)
            res = [jnp.dot(p_buf[slot, head], vt, preferred_element_type=F32)
                   for head in range(2)]
            denom = jnp.where(head0, res[0][:, LANES:], res[1][:, LANES:])
            out_p[q_rows, :] = jnp.where(head0, res[0][:, :LANES], res[1][:, :LANES]) / denom
            lse_p[q_rows, :] = m_buf[slot] + jnp.log(denom) * LOG2_E

        for u in range(ATTN_UNROLL):
            probs(u, u)

        def group(g, carry, probs=probs, weighted_values=weighted_values):
            for u in range(ATTN_UNROLL):
                weighted_values(g * ATTN_UNROLL + u, u)
            for u in range(ATTN_UNROLL):
                probs((g + 1) * ATTN_UNROLL + u, u)
            return carry

        n_groups = n_tiles // ATTN_UNROLL
        lax.fori_loop(0, n_groups - 1, group, 0)
        for u in range(ATTN_UNROLL):
            weighted_values((n_groups - 1) * ATTN_UNROLL + u, u)

    def epilogue(t, carry):
        rows = pl.ds(pl.multiple_of(t * SPAN, SPAN), SPAN)
        a0, a1, a2 = lse0[rows, :], lse1[rows, :], lse2[rows, :]
        top = jnp.maximum(jnp.maximum(a0, a1), a2)
        w0, w1, w2 = jnp.exp2(a0 - top), jnp.exp2(a1 - top), jnp.exp2(a2 - top)
        o_ref[rows, :] = ((w0 * out0[rows, :] + w1 * out1[rows, :] + w2 * out2[rows, :])
                          / (w0 + w1 + w2))
        return carry

    lax.fori_loop(0, seq // SPAN, epilogue, 0)


def _dilated_attention(proj3d, cos, sin):
    b, seq, _ = proj3d.shape
    n_pairs = ATTN_WIDTH // LANES
    blk = lambda off: pl.BlockSpec((None, seq, LANES), lambda i, j: (i, 0, off + j))
    tab = pl.BlockSpec((seq, LANES), lambda i, j: (0, 0))
    slab = pltpu.VMEM((seq, LANES), F32)
    padded = pltpu.VMEM((KEY_PAD + seq, LANES), F32)
    return pl.pallas_call(
        functools.partial(_attn_kernel, seq=seq),
        grid=(b, n_pairs),
        in_specs=[blk(0), blk(n_pairs), blk(2 * n_pairs), tab, tab],
        out_specs=pl.BlockSpec((None, seq, LANES), lambda i, j: (i, 0, j)),
        out_shape=jax.ShapeDtypeStruct((b, seq, ATTN_WIDTH), F32),
        scratch_shapes=[slab, padded, padded] + [slab] * 6 + [
            pltpu.VMEM((ATTN_UNROLL, 2, SPAN, 2 * SPAN), BF16),
            pltpu.VMEM((ATTN_UNROLL, SPAN, LANES), F32)],
        compiler_params=pltpu.CompilerParams(
            dimension_semantics=("arbitrary", "arbitrary"),
            vmem_limit_bytes=V7X_VMEM_LIMIT),
        name="dilated_attn",
    )(proj3d, proj3d, proj3d, cos, sin)


def _level_anchor(g, half):
    rows = g.shape[0]
    if 2 * half >= 8:
        g3 = g.reshape(rows // (2 * half), 2 * half, LANES)
        return jnp.broadcast_to(g3[:, half - 1:half, :], g3.shape).reshape(rows, LANES)
    g3 = g.reshape(rows // 8, 8, LANES)
    sub = lax.broadcasted_iota(jnp.int32, g3.shape, 1)
    out = jnp.broadcast_to(g3[:, half - 1:half, :], g3.shape)
    for first in range(2 * half, 8, 2 * half):
        src = jnp.broadcast_to(g3[:, first + half - 1:first + half, :], g3.shape)
        out = jnp.where(sub >= first, src, out)
    return out.reshape(rows, LANES)


def _chunk_cumsum(x):
    row = lax.broadcasted_iota(jnp.int32, x.shape, 0)
    shift = 1
    while shift < x.shape[0]:
        x = x + jnp.where(row >= shift, pltpu.roll(x, shift, axis=0), 0.0)
        shift *= 2
    return x


def _hgrn_kernel(q_ref, f_ref, i_ref, g_ref, lbl_ref, gain_ref, lvl_ref, o_ref, state,
                 *, layer):
    @pl.when(pl.program_id(2) == 0)
    def _():
        state[...] = jnp.zeros_like(state)

    logits = lbl_ref[...]
    e = jnp.exp(logits - jnp.max(logits, axis=0, keepdims=True))
    share = e / jnp.sum(e, axis=0, keepdims=True)
    lb = jnp.sum(share[0:layer + 1, :], axis=0, keepdims=True) - share[0:1, :]

    level = lvl_ref[...]
    n_levels = int(math.log2(CHUNK))
    odd_row = (lax.broadcasted_iota(jnp.int32, (CHUNK, LANES), 0) & 1) == 1
    for c0 in range(0, q_ref.shape[0], CHUNK):
        sl = slice(c0, c0 + CHUNK)
        qx = q_ref[sl, :]
        q = qx * jax.nn.sigmoid(qx) * (HGRN_HEAD_DIM ** -0.5)
        z = f_ref[sl, :]
        v = i_ref[sl, :].astype(BF16)
        t = jnp.exp(-jnp.abs(z))
        r = 1.0 / (1.0 + t)
        sig_pos = jnp.where(z >= 0, r, t * r)
        sig_neg = jnp.where(z >= 0, t * r, r)
        f = lb + (1.0 - lb) * sig_pos
        k = (1.0 - lb) * sig_neg
        g = _chunk_cumsum(jnp.log(f))
        g_last = g[CHUNK - 1:CHUNK, :]

        a = lax.dot_general(q.astype(BF16), k.astype(BF16), (((1,), (1,)), ((), ())),
                            preferred_element_type=F32)
        a = jnp.where(level == n_levels, a, 0.0)
        for lv in range(n_levels):
            half = 1 << lv
            if lv == 0:
                x = jnp.where(odd_row, f, 1.0)
            else:
                x = jnp.exp(-jnp.abs(g - _level_anchor(g, half)))
            a_lv = lax.dot_general((q * x).astype(BF16), (k * x).astype(BF16),
                                   (((1,), (1,)), ((), ())), preferred_element_type=F32)
            a = jnp.where(level == lv, a_lv, a)
        o = jnp.dot(a.astype(BF16), v, preferred_element_type=F32)

        st = state[...]
        q_in = (q * jnp.exp(g)).astype(BF16)
        o = o + lax.dot_general(q_in, st.astype(BF16), (((1,), (1,)), ((), ())),
                                preferred_element_type=F32)
        k_out = (k * jnp.exp(g_last - g)).astype(BF16)
        state[...] = st * jnp.exp(g_last) + lax.dot_general(
            v, k_out, (((0,), (0,)), ((), ())), preferred_element_type=F32)

        gx = g_ref[sl, :]
        o_ref[sl, :] = _rms(o, gain_ref[...]) * (gx * jax.nn.sigmoid(gx))


def _hgrn2(proj3d, lb_logits, out_gain, level_map, layer):
    b, seq, _ = proj3d.shape
    base = 3 * ATTN_WIDTH // LANES
    blk = lambda off: pl.BlockSpec((None, HGRN_STEP, LANES),
                                   lambda i, h, s: (i, s, base + off * HGRN_HEADS + h))
    depth = lb_logits.shape[0]
    return pl.pallas_call(
        functools.partial(_hgrn_kernel, layer=layer),
        grid=(b, HGRN_HEADS, seq // HGRN_STEP),
        in_specs=[
            blk(0), blk(1), blk(2), blk(3),
            pl.BlockSpec((depth, LANES), lambda i, h, s: (0, h)),
            pl.BlockSpec((1, LANES), lambda i, h, s: (0, 0)),
            pl.BlockSpec((CHUNK, CHUNK), lambda i, h, s: (0, 0)),
        ],
        out_specs=pl.BlockSpec((None, HGRN_STEP, LANES), lambda i, h, s: (i, s, h)),
        out_shape=jax.ShapeDtypeStruct((b, seq, HGRN_WIDTH), F32),
        scratch_shapes=[pltpu.VMEM((HGRN_HEAD_DIM, HGRN_HEAD_DIM), F32)],
        compiler_params=pltpu.CompilerParams(
            dimension_semantics=("arbitrary", "arbitrary", "arbitrary"),
            vmem_limit_bytes=V7X_VMEM_LIMIT),
        name="hgrn2",
    )(proj3d, proj3d, proj3d, proj3d, lb_logits, out_gain.reshape(1, LANES), level_map)


def _intra_chunk_levels():
    i = jnp.arange(CHUNK, dtype=jnp.int32)[:, None]
    j = jnp.arange(CHUNK, dtype=jnp.int32)[None, :]
    x = i ^ j
    lv = jnp.zeros((CHUNK, CHUNK), jnp.int32)
    for bit in range(1, int(math.log2(CHUNK))):
        lv = jnp.where(x >= (1 << bit), bit, lv)
    lv = jnp.where(i == j, int(math.log2(CHUNK)), lv)
    return jnp.where(j > i, -1, lv)


def _mix_mlp_kernel(x_ref, attn_ref, rec_ref, ag_ref, wo_ref, nm_ref, wu_ref, wd_ref,
                    nf_ref, o_ref, *, final_norm):
    attn = _rms(attn_ref[...], ag_ref[...]).astype(BF16)
    x1 = (x_ref[...]
          + jnp.dot(attn, wo_ref[0:ATTN_WIDTH, :], preferred_element_type=F32)
          + jnp.dot(rec_ref[...].astype(BF16), wo_ref[ATTN_WIDTH:MIX_WIDTH, :],
                    preferred_element_type=F32))
    h = _rms(x1, nm_ref[...]).astype(BF16)
    o_ref[...] = x1
    for c0 in range(0, MLP_HIDDEN, COL_CHUNK):
        u = jnp.dot(h, wu_ref[:, c0:c0 + COL_CHUNK], preferred_element_type=F32)
        u = jnp.square(jnp.maximum(u, 0.0)).astype(BF16)
        o_ref[...] += jnp.dot(u, wd_ref[c0:c0 + COL_CHUNK, :], preferred_element_type=F32)
    if final_norm:
        o_ref[...] = _rms(o_ref[...], nf_ref[...])


def _mix_mlp(x2d, attn2d, rec2d, attn_gain, w_out, norm_mlp, w_up, w_down, norm_final,
             final_norm):
    t, d = x2d.shape
    row = lambda w: pl.BlockSpec((ROW_TILE, w), lambda i: (i, 0))
    full = lambda a: pl.BlockSpec(a.shape, lambda i: (0, 0))
    vec = lambda a: a.reshape(1, -1)
    args = (x2d, attn2d, rec2d, vec(attn_gain), w_out, vec(norm_mlp), w_up, w_down,
            vec(norm_final))
    return pl.pallas_call(
        functools.partial(_mix_mlp_kernel, final_norm=final_norm),
        grid=(t // ROW_TILE,),
        in_specs=[row(d), row(ATTN_WIDTH), row(HGRN_WIDTH)] + [full(a) for a in args[3:]],
        out_specs=row(d),
        out_shape=jax.ShapeDtypeStruct((t, d), F32),
        compiler_params=pltpu.CompilerParams(
            dimension_semantics=("arbitrary",), vmem_limit_bytes=V7X_VMEM_LIMIT),
        name="mix_mlp",
    )(*args)


def _rotary_tables(seq):
    half = ATTN_HEAD_DIM // 2
    inv_freq = ROPE_THETA ** (-jnp.arange(half, dtype=F32) / half)
    ang = jnp.arange(seq, dtype=F32)[:, None] * inv_freq[None, :]
    cos, sin = jnp.cos(ang), jnp.sin(ang)
    return (jnp.concatenate([cos, cos, cos, cos], axis=1),
            jnp.concatenate([-sin, -sin, sin, sin], axis=1))


def _pair_rotary_halves(w_in_layer):
    d = w_in_layer.shape[0]
    half = ATTN_HEAD_DIM // 2
    qk = w_in_layer[:, :2 * ATTN_WIDTH].reshape(d, 2 * ATTN_WIDTH // LANES, 2, 2, half)
    qk = qk.transpose(0, 1, 3, 2, 4).reshape(d, 2 * ATTN_WIDTH)
    return jnp.concatenate([qk, w_in_layer[:, 2 * ATTN_WIDTH:]], axis=1)


def kernel(x, norm_mix, w_in, attn_out_gain, hgrn_lb_logits, hgrn_out_gain, w_out, norm_mlp,
           w_up, w_down, norm_final):
    b, seq, d = x.shape
    depth = w_in.shape[0]
    cos, sin = _rotary_tables(seq)
    level_map = _intra_chunk_levels()
    x2d = x.reshape(b * seq, d)
    for layer in range(depth):
        proj = _norm_in_proj(x2d, norm_mix[layer],
                             _pair_rotary_halves(w_in[layer]).astype(BF16))
        proj3d = proj.reshape(b, seq, IN_PROJ_WIDTH)
        attn = _dilated_attention(proj3d, cos, sin)
        rec = _hgrn2(proj3d, hgrn_lb_logits, hgrn_out_gain[layer], level_map, layer)
        x2d = _mix_mlp(x2d, attn.reshape(b * seq, ATTN_WIDTH), rec.reshape(b * seq, HGRN_WIDTH),
                       attn_out_gain[layer], w_out[layer].astype(BF16), norm_mlp[layer],
                       w_up[layer].astype(BF16), w_down[layer].astype(BF16), norm_final,
                       final_norm=(layer == depth - 1))
    return x2d.reshape(b, seq, d)
```

```python
import functools
import math

import jax
import jax.numpy as jnp
from jax import lax
from jax.experimental import pallas as pl
from jax.experimental.pallas import tpu as pltpu

F32 = jnp.float32
BF16 = jnp.bfloat16

D_MODEL = 1024
ATTN_HEADS = 8
ATTN_HEAD_DIM = 64
ATTN_WIDTH = ATTN_HEADS * ATTN_HEAD_DIM
HGRN_HEADS = 4
HGRN_HEAD_DIM = 128
HGRN_WIDTH = HGRN_HEADS * HGRN_HEAD_DIM
MIX_WIDTH = ATTN_WIDTH + HGRN_WIDTH
IN_PROJ_WIDTH = 3 * ATTN_WIDTH + 4 * HGRN_WIDTH
DILATIONS = (1, 4, 16)
SPAN = 128
ROPE_THETA = 10000.0
MLP_HIDDEN = 4 * D_MODEL
NORM_EPS = 1e-6
MASK_VALUE = -1e30

LANES = 128
V7X_VMEM_LIMIT = 56 * 1024 * 1024
ROW_TILE = 512
COL_CHUNK = 512
CHUNK = 128
KEY_PAD = max(DILATIONS) * SPAN
ATTN_UNROLL = 4
PROJ_PIECE = 256
ROPE_UNROLL = 4
LOG2_E = 1.4426950408889634


def _rms(x, gain):
    return x * lax.rsqrt(jnp.mean(x * x, axis=-1, keepdims=True) + NORM_EPS) * gain


def _attn_kernel(q_ref, k_ref, v_ref, cos_ref, sin_ref, o_ref,
                 qr, kr, vr, out0, out1, out2, lse0, lse1, lse2, p_buf, m_buf, *, seq):
    def prologue(t, carry):
        for u in range(ROPE_UNROLL):
            start = (t * ROPE_UNROLL + u) * SPAN
            rows = pl.ds(pl.multiple_of(start, SPAN), SPAN)
            pad_rows = pl.ds(pl.multiple_of(KEY_PAD + start, SPAN), SPAN)
            cos = cos_ref[rows, :]
            sin = sin_ref[rows, :]
            q = q_ref[rows, :]
            k = k_ref[rows, :]
            qr[rows, :] = ((q * cos + pltpu.roll(q, LANES // 2, axis=1) * sin)
                           * (ATTN_HEAD_DIM ** -0.5 * LOG2_E))
            kr[pad_rows, :] = k * cos + pltpu.roll(k, LANES // 2, axis=1) * sin
            vr[pad_rows, :] = v_ref[rows, :]
        return carry

    lax.fori_loop(0, seq // (SPAN * ROPE_UNROLL), prologue, 0)
    kr[0:KEY_PAD, :] = jnp.zeros((KEY_PAD, LANES), F32)
    vr[0:KEY_PAD, :] = jnp.zeros((KEY_PAD, LANES), F32)

    lane = lax.broadcasted_iota(jnp.int32, (SPAN, LANES), 1)
    head0 = lane < ATTN_HEAD_DIM
    qk_head0 = (lane & (ATTN_HEAD_DIM // 2)) == 0
    q_minus_k = (lax.broadcasted_iota(jnp.int32, (SPAN, SPAN), 0)
                 - lax.broadcasted_iota(jnp.int32, (SPAN, SPAN), 1))
    ones = jnp.ones((2 * SPAN, LANES), BF16)
    n_tiles = seq // SPAN

    for dil, out_p, lse_p in zip(DILATIONS, (out0, out1, out2), (lse0, lse1, lse2)):
        n_blk = n_tiles // dil

        def rows(start, size, dil=dil):
            return pl.ds(start, size) if dil == 1 else pl.ds(start, size, stride=dil)

        def tile_rows(t, dil=dil, n_blk=n_blk, rows=rows):
            r = t // n_blk
            n = t % n_blk
            return (n, rows(r + dil * SPAN * n, SPAN),
                    rows(KEY_PAD + r + dil * SPAN * (n - 1), 2 * SPAN))

        def probs(t, slot, tile_rows=tile_rows):
            n, q_rows, k_rows = tile_rows(t)
            qt = qr[q_rows, :]
            kt = kr[k_rows, :].astype(BF16)
            prev_ok = q_minus_k <= jnp.where(n > 0, 0, -SPAN - 1)
            cur_ok = q_minus_k >= 0
            scores = []
            for head in range(2):
                qh = jnp.where(qk_head0 if head == 0 else ~qk_head0, qt, 0.0).astype(BF16)
                scores.append(lax.dot_general(qh, kt, (((1,), (1,)), ((), ())),
                                              preferred_element_type=F32))
            mx = []
            for head, s in enumerate(scores):
                s_prev = jnp.where(prev_ok, s[:, :SPAN], MASK_VALUE)
                s_cur = jnp.where(cur_ok, s[:, SPAN:], MASK_VALUE)
                m = jnp.max(jnp.maximum(s_prev, s_cur), axis=-1, keepdims=True)
                p_buf[slot, head, :, 0:SPAN] = jnp.exp2(s_prev - m).astype(BF16)
                p_buf[slot, head, :, SPAN:2 * SPAN] = jnp.exp2(s_cur - m).astype(BF16)
                mx.append(jnp.broadcast_to(m, (SPAN, LANES)))
            m_buf[slot] = jnp.where(head0, mx[0], mx[1])

        def weighted_values(t, slot, tile_rows=tile_rows, out_p=out_p, lse_p=lse_p):
            _, q_rows, k_rows = tile_rows(t)
            vt = jnp.concatenate([vr[k_rows, :].astype(BF16), ones], axis=1)
            res = [jnp.dot(p_buf[slot, head], vt, preferred_element_type=F32)
                   for head in range(2)]
            denom = jnp.where(head0, res[0][:, LANES:], res[1][:, LANES:])
            out_p[q_rows, :] = jnp.where(head0, res[0][:, :LANES], res[1][:, :LANES]) / denom
            lse_p[q_rows, :] = m_buf[slot] + jnp.log(denom) * LOG2_E

        for u in range(ATTN_UNROLL):
            probs(u, u)

        def group(g, carry, probs=probs, weighted_values=weighted_values):
            for u in range(ATTN_UNROLL):
                weighted_values(g * ATTN_UNROLL + u, u)
            for u in range(ATTN_UNROLL):
                probs((g + 1) * ATTN_UNROLL + u, u)
            return carry

        n_groups = n_tiles // ATTN_UNROLL
        lax.fori_loop(0, n_groups - 1, group, 0)
        for u in range(ATTN_UNROLL):
            weighted_values((n_groups - 1) * ATTN_UNROLL + u, u)

    def epilogue(t, carry):
        rows = pl.ds(pl.multiple_of(t * SPAN, SPAN), SPAN)
        a0, a1, a2 = lse0[rows, :], lse1[rows, :], lse2[rows, :]
        top = jnp.maximum(jnp.maximum(a0, a1), a2)
        w0, w1, w2 = jnp.exp2(a0 - top), jnp.exp2(a1 - top), jnp.exp2(a2 - top)
        o_ref[rows, :] = ((w0 * out0[rows, :] + w1 * out1[rows, :] + w2 * out2[rows, :])
                          / (w0 + w1 + w2))
        return carry

    lax.fori_loop(0, seq // SPAN, epilogue, 0)


def _dilated_attention(proj3d, cos, sin):
    b, seq, _ = proj3d.shape
    n_pairs = ATTN_WIDTH // LANES
    blk = lambda off: pl.BlockSpec((None, seq, LANES), lambda i, j: (i, 0, off + j))
    tab = pl.BlockSpec((seq, LANES), lambda i, j: (0, 0))
    slab = pltpu.VMEM((seq, LANES), F32)
    padded = pltpu.VMEM((KEY_PAD + seq, LANES), F32)
    return pl.pallas_call(
        functools.partial(_attn_kernel, seq=seq),
        grid=(b, n_pairs),
        in_specs=[blk(0), blk(n_pairs), blk(2 * n_pairs), tab, tab],
        out_specs=pl.BlockSpec((None, seq, LANES), lambda i, j: (i, 0, j)),
        out_shape=jax.ShapeDtypeStruct((b, seq, ATTN_WIDTH), F32),
        scratch_shapes=[slab, padded, padded] + [slab] * 6 + [
            pltpu.VMEM((ATTN_UNROLL, 2, SPAN, 2 * SPAN), BF16),
            pltpu.VMEM((ATTN_UNROLL, SPAN, LANES), F32)],
        compiler_params=pltpu.CompilerParams(
            dimension_semantics=("arbitrary", "arbitrary"),
            vmem_limit_bytes=V7X_VMEM_LIMIT),
        name="dilated_attn",
    )(proj3d, proj3d, proj3d, cos, sin)


def _level_anchor(g, half):
    rows = g.shape[0]
    if 2 * half >= 8:
        g3 = g.reshape(rows // (2 * half), 2 * half, LANES)
        return jnp.broadcast_to(g3[:, half - 1:half, :], g3.shape).reshape(rows, LANES)
    g3 = g.reshape(rows // 8, 8, LANES)
    sub = lax.broadcasted_iota(jnp.int32, g3.shape, 1)
    out = jnp.broadcast_to(g3[:, half - 1:half, :], g3.shape)
    for first in range(2 * half, 8, 2 * half):
        src = jnp.broadcast_to(g3[:, first + half - 1:first + half, :], g3.shape)
        out = jnp.where(sub >= first, src, out)
    return out.reshape(rows, LANES)


def _neg_abs(x):
    return -jnp.abs(x)


def _dot_nt(a, b):
    return lax.dot_general(a, b, (((1,), (1,)), ((), ())), preferred_element_type=F32)


def _chunk_cumsum(x):
    row = lax.broadcasted_iota(jnp.int32, x.shape, 0)
    shift = 1
    while shift < x.shape[0]:
        x = x + jnp.where(row >= shift, pltpu.roll(x, shift, axis=0), 0.0)
        shift *= 2
    return x


def _in_proj_hgrn_kernel(x_ref, nm_ref, w_ref, lbl_ref, gain_ref, lvl_ref,
                         qkv_ref, rec_ref, hin, state, *, layer, tiles_per_seq):
    h = _rms(x_ref[...], nm_ref[...]).astype(BF16)
    head_cols = 4 * HGRN_HEAD_DIM
    restart = pl.program_id(0) % tiles_per_seq == 0

    logits = lbl_ref[...]
    e = jnp.exp(logits - jnp.max(logits, axis=0, keepdims=True))
    share = e / jnp.sum(e, axis=0, keepdims=True)
    lb_all = jnp.sum(share[0:layer + 1, :], axis=0, keepdims=True) - share[0:1, :]

    n_rec = HGRN_HEADS * head_cols
    pieces = [(hin, c, c) for c in range(0, n_rec, PROJ_PIECE)]
    pieces += [(qkv_ref, c, n_rec + c) for c in range(0, 3 * ATTN_WIDTH, PROJ_PIECE)]

    def project(n_pieces):
        for _ in range(n_pieces):
            dst, c_dst, c_w = pieces.pop(0)
            dst[:, c_dst:c_dst + PROJ_PIECE] = jnp.dot(
                h, w_ref[:, c_w:c_w + PROJ_PIECE], preferred_element_type=F32)

    project(head_cols // PROJ_PIECE)
    n_spread = len(pieces)
    chunks_per_head = x_ref.shape[0] // CHUNK
    n_slots = HGRN_HEADS * chunks_per_head
    level = lvl_ref[...]
    odd_row = (lax.broadcasted_iota(jnp.int32, (CHUNK, LANES), 0) & 1) == 1
    for head in range(HGRN_HEADS):
        c0 = head * head_cols
        lb = lb_all[:, head * HGRN_HEAD_DIM:(head + 1) * HGRN_HEAD_DIM]
        st = jnp.where(restart, 0.0, state[head])
        for c in range(chunks_per_head):
            r0 = c * CHUNK
            cols = lambda j: hin[r0:r0 + CHUNK, c0 + j * LANES:c0 + (j + 1) * LANES]
            out, st = _hgrn_chunk(cols(0), cols(1), cols(2), cols(3), st, lb,
                                  gain_ref[...], level, odd_row)
            rec_ref[r0:r0 + CHUNK, head * HGRN_HEAD_DIM:(head + 1) * HGRN_HEAD_DIM] = out
            slot = head * chunks_per_head + c
            due = -(-(slot + 1) * n_spread // n_slots)
            project(due - (n_spread - len(pieces)))
        state[head] = st
    assert not pieces


def _hgrn_chunk(qx, z, vx, gx, st, lb, gain, level, odd_row):
    n_levels = int(math.log2(CHUNK))
    q = qx * jax.nn.sigmoid(qx) * (HGRN_HEAD_DIM ** -0.5)
    v_t = vx.T.astype(BF16)
    t = jnp.exp2(_neg_abs(z) * LOG2_E)
    r = 1.0 / (1.0 + t)
    sig_pos = jnp.where(z >= 0, r, t * r)
    sig_neg = jnp.where(z >= 0, t * r, r)
    f = lb + (1.0 - lb) * sig_pos
    k = (1.0 - lb) * sig_neg
    g = _chunk_cumsum(jnp.log(f) * LOG2_E)
    g_last = g[CHUNK - 1:CHUNK, :]

    q_bf, k_bf = q.astype(BF16), k.astype(BF16)
    a = _dot_nt(q_bf, k_bf)
    a = jnp.where(level == n_levels, a, 0.0)
    for lv in range(n_levels):
        if lv == 0:
            x = jnp.where(odd_row, f, 1.0)
        else:
            x = jnp.exp2(_neg_abs(g - _level_anchor(g, 1 << lv)))
        x = x.astype(BF16)
        a = jnp.where(level == lv, _dot_nt(q_bf * x, k_bf * x), a)

    lhs = jnp.concatenate([a.astype(BF16), (q * jnp.exp2(g)).astype(BF16)], axis=1)
    rhs_t = jnp.concatenate([v_t, st.astype(BF16)], axis=1)
    o = _dot_nt(lhs, rhs_t)
    k_out = (k * jnp.exp2(g_last - g)).astype(BF16)
    st = st * jnp.exp2(g_last) + jnp.dot(v_t, k_out, preferred_element_type=F32)
    return _rms(o, gain) * (gx * jax.nn.sigmoid(gx)), st


def _in_proj_hgrn(x2d, norm_gain, w_bf16, lb_logits, out_gain, level_map, layer, seq):
    t, d = x2d.shape
    n = w_bf16.shape[1]
    full = lambda a: pl.BlockSpec(a.shape, lambda i: (0,) * a.ndim)
    row = lambda width: pl.BlockSpec((ROW_TILE, width), lambda i: (i, 0))
    consts = (norm_gain.reshape(1, d), w_bf16, lb_logits, out_gain.reshape(1, LANES), level_map)
    return pl.pallas_call(
        functools.partial(_in_proj_hgrn_kernel, layer=layer, tiles_per_seq=seq // ROW_TILE),
        grid=(t // ROW_TILE,),
        in_specs=[row(d)] + [full(a) for a in consts],
        out_specs=[row(3 * ATTN_WIDTH), row(HGRN_WIDTH)],
        out_shape=[jax.ShapeDtypeStruct((t, 3 * ATTN_WIDTH), F32),
                   jax.ShapeDtypeStruct((t, HGRN_WIDTH), F32)],
        scratch_shapes=[pltpu.VMEM((ROW_TILE, n - 3 * ATTN_WIDTH), F32),
                        pltpu.VMEM((HGRN_HEADS, HGRN_HEAD_DIM, HGRN_HEAD_DIM), F32)],
        compiler_params=pltpu.CompilerParams(
            dimension_semantics=("arbitrary",), vmem_limit_bytes=V7X_VMEM_LIMIT),
        name="in_proj_hgrn",
    )(x2d, *consts)


def _intra_chunk_levels():
    i = jnp.arange(CHUNK, dtype=jnp.int32)[:, None]
    j = jnp.arange(CHUNK, dtype=jnp.int32)[None, :]
    x = i ^ j
    lv = jnp.zeros((CHUNK, CHUNK), jnp.int32)
    for bit in range(1, int(math.log2(CHUNK))):
        lv = jnp.where(x >= (1 << bit), bit, lv)
    lv = jnp.where(i == j, int(math.log2(CHUNK)), lv)
    return jnp.where(j > i, -1, lv)


def _mix_mlp_kernel(x_ref, attn_ref, rec_ref, ag_ref, wo_ref, nm_ref, wu_ref, wd_ref,
                    nf_ref, o_ref, *, final_norm):
    attn = _rms(attn_ref[...], ag_ref[...]).astype(BF16)
    x1 = (x_ref[...]
          + jnp.dot(attn, wo_ref[0:ATTN_WIDTH, :], preferred_element_type=F32)
          + jnp.dot(rec_ref[...].astype(BF16), wo_ref[ATTN_WIDTH:MIX_WIDTH, :],
                    preferred_element_type=F32))
    h = _rms(x1, nm_ref[...]).astype(BF16)
    o_ref[...] = x1
    for c0 in range(0, MLP_HIDDEN, COL_CHUNK):
        u = jnp.dot(h, wu_ref[:, c0:c0 + COL_CHUNK], preferred_element_type=F32)
        u = jnp.square(jnp.maximum(u, 0.0)).astype(BF16)
        o_ref[...] += jnp.dot(u, wd_ref[c0:c0 + COL_CHUNK, :], preferred_element_type=F32)
    if final_norm:
        o_ref[...] = _rms(o_ref[...], nf_ref[...])


def _mix_mlp(x2d, attn2d, rec2d, attn_gain, w_out, norm_mlp, w_up, w_down, norm_final,
             final_norm):
    t, d = x2d.shape
    row = lambda w: pl.BlockSpec((ROW_TILE, w), lambda i: (i, 0))
    full = lambda a: pl.BlockSpec(a.shape, lambda i: (0, 0))
    vec = lambda a: a.reshape(1, -1)
    args = (x2d, attn2d, rec2d, vec(attn_gain), w_out, vec(norm_mlp), w_up, w_down,
            vec(norm_final))
    return pl.pallas_call(
        functools.partial(_mix_mlp_kernel, final_norm=final_norm),
        grid=(t // ROW_TILE,),
        in_specs=[row(d), row(ATTN_WIDTH), row(HGRN_WIDTH)] + [full(a) for a in args[3:]],
        out_specs=row(d),
        out_shape=jax.ShapeDtypeStruct((t, d), F32),
        compiler_params=pltpu.CompilerParams(
            dimension_semantics=("arbitrary",), vmem_limit_bytes=V7X_VMEM_LIMIT),
        name="mix_mlp",
    )(*args)


def _rotary_tables(seq):
    half = ATTN_HEAD_DIM // 2
    inv_freq = ROPE_THETA ** (-jnp.arange(half, dtype=F32) / half)
    ang = jnp.arange(seq, dtype=F32)[:, None] * inv_freq[None, :]
    cos, sin = jnp.cos(ang), jnp.sin(ang)
    return (jnp.concatenate([cos, cos, cos, cos], axis=1),
            jnp.concatenate([-sin, -sin, sin, sin], axis=1))


def _arrange_in_proj_columns(w_in_layer):
    d = w_in_layer.shape[0]
    half = ATTN_HEAD_DIM // 2
    qk = w_in_layer[:, :2 * ATTN_WIDTH].reshape(d, 2 * ATTN_WIDTH // LANES, 2, 2, half)
    qk = qk.transpose(0, 1, 3, 2, 4).reshape(d, 2 * ATTN_WIDTH)
    v = w_in_layer[:, 2 * ATTN_WIDTH:3 * ATTN_WIDTH]
    rec = w_in_layer[:, 3 * ATTN_WIDTH:].reshape(d, 4, HGRN_HEADS, HGRN_HEAD_DIM)
    rec = rec.transpose(0, 2, 1, 3).reshape(d, 4 * HGRN_WIDTH)
    return jnp.concatenate([rec, qk, v], axis=1)


def kernel(x, norm_mix, w_in, attn_out_gain, hgrn_lb_logits, hgrn_out_gain, w_out, norm_mlp,
           w_up, w_down, norm_final):
    b, seq, d = x.shape
    depth = w_in.shape[0]
    cos, sin = _rotary_tables(seq)
    level_map = _intra_chunk_levels()
    x2d = x.reshape(b * seq, d)
    for layer in range(depth):
        qkv, rec = _in_proj_hgrn(x2d, norm_mix[layer],
                                 _arrange_in_proj_columns(w_in[layer]).astype(BF16),
                                 hgrn_lb_logits, hgrn_out_gain[layer], level_map, layer, seq)
        attn = _dilated_attention(qkv.reshape(b, seq, 3 * ATTN_WIDTH), cos, sin)
        x2d = _mix_mlp(x2d, attn.reshape(b * seq, ATTN_WIDTH), rec,
                       attn_out_gain[layer], w_out[layer].astype(BF16), norm_mlp[layer],
                       w_up[layer].astype(BF16), w_down[layer].astype(BF16), norm_final,
                       final_norm=(layer == depth - 1))
    return x2d.reshape(b, seq, d)
```

```python
import functools
import math

import jax
import jax.numpy as jnp
from jax import lax
from jax.experimental import pallas as pl
from jax.experimental.pallas import tpu as pltpu

F32 = jnp.float32
BF16 = jnp.bfloat16

D_MODEL = 1024
ATTN_HEADS = 8
ATTN_HEAD_DIM = 64
ATTN_WIDTH = ATTN_HEADS * ATTN_HEAD_DIM
HGRN_HEADS = 4
HGRN_HEAD_DIM = 128
HGRN_WIDTH = HGRN_HEADS * HGRN_HEAD_DIM
MIX_WIDTH = ATTN_WIDTH + HGRN_WIDTH
IN_PROJ_WIDTH = 3 * ATTN_WIDTH + 4 * HGRN_WIDTH
DILATIONS = (1, 4, 16)
SPAN = 128
ROPE_THETA = 10000.0
MLP_HIDDEN = 4 * D_MODEL
NORM_EPS = 1e-6
MASK_VALUE = -1e30

LANES = 128
V7X_VMEM_LIMIT = 56 * 1024 * 1024
ROW_TILE = 512
COL_CHUNK = 512
CHUNK = 128
ATTN_UNROLL = 4
PROJ_PIECE = 256
ROPE_UNROLL = 4
LOG2_E = 1.4426950408889634


def _rms(x, gain):
    return x * lax.rsqrt(jnp.mean(x * x, axis=-1, keepdims=True) + NORM_EPS) * gain


def _attn_kernel(q_ref, k_ref, v_ref, cos_ref, sin_ref, o_ref,
                 qr, kr, out0, out1, out2, lse0, lse1, lse2, p_buf, m_buf, *, seq):
    def prologue(t, carry):
        for u in range(ROPE_UNROLL):
            rows = pl.ds(pl.multiple_of((t * ROPE_UNROLL + u) * SPAN, SPAN), SPAN)
            cos = cos_ref[rows, :]
            sin = sin_ref[rows, :]
            q = q_ref[rows, :]
            k = k_ref[rows, :]
            qr[rows, :] = ((q * cos + pltpu.roll(q, LANES // 2, axis=1) * sin)
                           * (ATTN_HEAD_DIM ** -0.5 * LOG2_E))
            kr[rows, :] = k * cos + pltpu.roll(k, LANES // 2, axis=1) * sin
        return carry

    lax.fori_loop(0, seq // (SPAN * ROPE_UNROLL), prologue, 0)

    lane = lax.broadcasted_iota(jnp.int32, (SPAN, LANES), 1)
    head0 = lane < ATTN_HEAD_DIM
    qk_head0 = (lane & (ATTN_HEAD_DIM // 2)) == 0
    q_minus_k = (lax.broadcasted_iota(jnp.int32, (SPAN, SPAN), 0)
                 - lax.broadcasted_iota(jnp.int32, (SPAN, SPAN), 1))
    prev_ok = q_minus_k <= 0
    cur_ok = q_minus_k >= 0
    ones = jnp.ones((SPAN, LANES), BF16)
    n_tiles = seq // SPAN

    for dil, out_p, lse_p in zip(DILATIONS, (out0, out1, out2), (lse0, lse1, lse2)):
        n_blk = n_tiles // dil
        tiles = [(r, n) for r in range(dil) for n in range(n_blk)]

        def block_rows(r, n, dil=dil):
            start = r + dil * SPAN * n
            return pl.ds(start, SPAN) if dil == 1 else pl.ds(start, SPAN, stride=dil)

        def probs(group, block_rows=block_rows):
            key_blocks = {}

            def keys(r, n):
                if (r, n) not in key_blocks:
                    key_blocks[r, n] = kr[block_rows(r, n), :].astype(BF16)
                return key_blocks[r, n]

            for slot, (r, n) in enumerate(group):
                qt = qr[block_rows(r, n), :]
                kt = keys(r, n) if n == 0 else jnp.concatenate([keys(r, n - 1), keys(r, n)], 0)
                scores = []
                for head in range(2):
                    qh = jnp.where(qk_head0 if head == 0 else ~qk_head0, qt, 0.0).astype(BF16)
                    scores.append(_dot_nt(qh, kt))
                mx = []
                for head, s in enumerate(scores):
                    s_cur = jnp.where(cur_ok, s[:, -SPAN:], MASK_VALUE)
                    if n == 0:
                        m = jnp.max(s_cur, axis=-1, keepdims=True)
                    else:
                        s_prev = jnp.where(prev_ok, s[:, :SPAN], MASK_VALUE)
                        m = jnp.max(jnp.maximum(s_prev, s_cur), axis=-1, keepdims=True)
                        p_buf[slot, head, :, 0:SPAN] = jnp.exp2(s_prev - m).astype(BF16)
                    p_buf[slot, head, :, SPAN:2 * SPAN] = jnp.exp2(s_cur - m).astype(BF16)
                    mx.append(jnp.broadcast_to(m, (SPAN, LANES)))
                m_buf[slot] = jnp.where(head0, mx[0], mx[1])

        def weighted_values(group, block_rows=block_rows, out_p=out_p, lse_p=lse_p):
            value_blocks = {}

            def values(r, n):
                if (r, n) not in value_blocks:
                    value_blocks[r, n] = jnp.concatenate(
                        [v_ref[block_rows(r, n), :].astype(BF16), ones], axis=1)
                return value_blocks[r, n]

            for slot, (r, n) in enumerate(group):
                if n == 0:
                    vt = values(r, n)
                    res = [jnp.dot(p_buf[slot, head, :, SPAN:2 * SPAN], vt,
                                   preferred_element_type=F32) for head in range(2)]
                else:
                    vt = jnp.concatenate([values(r, n - 1), values(r, n)], axis=0)
                    res = [jnp.dot(p_buf[slot, head], vt, preferred_element_type=F32)
                           for head in range(2)]
                denom = jnp.where(head0, res[0][:, LANES:], res[1][:, LANES:])
                q_rows = block_rows(r, n)
                out_p[q_rows, :] = (jnp.where(head0, res[0][:, :LANES], res[1][:, :LANES])
                                    / denom)
                lse_p[q_rows, :] = m_buf[slot] + jnp.log(denom) * LOG2_E

        groups = [tiles[i:i + ATTN_UNROLL] for i in range(0, len(tiles), ATTN_UNROLL)]
        probs(groups[0])
        for done, ahead in zip(groups[:-1], groups[1:]):
            weighted_values(done)
            probs(ahead)
        weighted_values(groups[-1])

    def epilogue(t, carry):
        rows = pl.ds(pl.multiple_of(t * SPAN, SPAN), SPAN)
        a0, a1, a2 = lse0[rows, :], lse1[rows, :], lse2[rows, :]
        top = jnp.maximum(jnp.maximum(a0, a1), a2)
        w0, w1, w2 = jnp.exp2(a0 - top), jnp.exp2(a1 - top), jnp.exp2(a2 - top)
        o_ref[rows, :] = ((w0 * out0[rows, :] + w1 * out1[rows, :] + w2 * out2[rows, :])
                          / (w0 + w1 + w2))
        return carry

    lax.fori_loop(0, seq // SPAN, epilogue, 0)


def _dilated_attention(proj3d, cos, sin):
    b, seq, _ = proj3d.shape
    n_pairs = ATTN_WIDTH // LANES
    blk = lambda off: pl.BlockSpec((None, seq, LANES), lambda i, j: (i, 0, off + j))
    tab = pl.BlockSpec((seq, LANES), lambda i, j: (0, 0))
    slab = pltpu.VMEM((seq, LANES), F32)
    return pl.pallas_call(
        functools.partial(_attn_kernel, seq=seq),
        grid=(b, n_pairs),
        in_specs=[blk(0), blk(n_pairs), blk(2 * n_pairs), tab, tab],
        out_specs=pl.BlockSpec((None, seq, LANES), lambda i, j: (i, 0, j)),
        out_shape=jax.ShapeDtypeStruct((b, seq, ATTN_WIDTH), F32),
        scratch_shapes=[slab] * 8 + [
            pltpu.VMEM((ATTN_UNROLL, 2, SPAN, 2 * SPAN), BF16),
            pltpu.VMEM((ATTN_UNROLL, SPAN, LANES), F32)],
        compiler_params=pltpu.CompilerParams(
            dimension_semantics=("arbitrary", "arbitrary"),
            vmem_limit_bytes=V7X_VMEM_LIMIT),
        name="dilated_attn",
    )(proj3d, proj3d, proj3d, cos, sin)


def _level_anchor(g, half):
    rows = g.shape[0]
    if 2 * half >= 8:
        g3 = g.reshape(rows // (2 * half), 2 * half, LANES)
        return jnp.broadcast_to(g3[:, half - 1:half, :], g3.shape).reshape(rows, LANES)
    g3 = g.reshape(rows // 8, 8, LANES)
    sub = lax.broadcasted_iota(jnp.int32, g3.shape, 1)
    out = jnp.broadcast_to(g3[:, half - 1:half, :], g3.shape)
    for first in range(2 * half, 8, 2 * half):
        src = jnp.broadcast_to(g3[:, first + half - 1:first + half, :], g3.shape)
        out = jnp.where(sub >= first, src, out)
    return out.reshape(rows, LANES)


def _neg_abs(x):
    return -jnp.abs(x)


def _dot_nt(a, b):
    return lax.dot_general(a, b, (((1,), (1,)), ((), ())), preferred_element_type=F32)


def _chunk_cumsum(x):
    row = lax.broadcasted_iota(jnp.int32, x.shape, 0)
    shift = 1
    while shift < x.shape[0]:
        x = x + jnp.where(row >= shift, pltpu.roll(x, shift, axis=0), 0.0)
        shift *= 2
    return x


def _in_proj_hgrn_kernel(x_ref, nm_ref, w_ref, lbl_ref, gain_ref, lvl_ref,
                         qkv_ref, rec_ref, hin, state, *, layer, tiles_per_seq):
    h = _rms(x_ref[...], nm_ref[...]).astype(BF16)
    head_cols = 4 * HGRN_HEAD_DIM
    restart = pl.program_id(0) % tiles_per_seq == 0

    logits = lbl_ref[...]
    e = jnp.exp(logits - jnp.max(logits, axis=0, keepdims=True))
    share = e / jnp.sum(e, axis=0, keepdims=True)
    lb_all = jnp.sum(share[0:layer + 1, :], axis=0, keepdims=True) - share[0:1, :]

    n_rec = HGRN_HEADS * head_cols
    pieces = [(hin, c, c) for c in range(0, n_rec, PROJ_PIECE)]
    pieces += [(qkv_ref, c, n_rec + c) for c in range(0, 3 * ATTN_WIDTH, PROJ_PIECE)]

    def project(n_pieces):
        for _ in range(n_pieces):
            dst, c_dst, c_w = pieces.pop(0)
            dst[:, c_dst:c_dst + PROJ_PIECE] = jnp.dot(
                h, w_ref[:, c_w:c_w + PROJ_PIECE], preferred_element_type=F32)

    project(head_cols // PROJ_PIECE)
    n_spread = len(pieces)
    chunks_per_head = x_ref.shape[0] // CHUNK
    n_slots = HGRN_HEADS * chunks_per_head
    level = lvl_ref[...]
    odd_row = (lax.broadcasted_iota(jnp.int32, (CHUNK, LANES), 0) & 1) == 1
    for head in range(HGRN_HEADS):
        c0 = head * head_cols
        lb = lb_all[:, head * HGRN_HEAD_DIM:(head + 1) * HGRN_HEAD_DIM]
        st = jnp.where(restart, 0.0, state[head])
        for c in range(chunks_per_head):
            r0 = c * CHUNK
            cols = lambda j: hin[r0:r0 + CHUNK, c0 + j * LANES:c0 + (j + 1) * LANES]
            out, st = _hgrn_chunk(cols(0), cols(1), cols(2), cols(3), st, lb,
                                  gain_ref[...], level, odd_row)
            rec_ref[r0:r0 + CHUNK, head * HGRN_HEAD_DIM:(head + 1) * HGRN_HEAD_DIM] = out
            slot = head * chunks_per_head + c
            due = -(-(slot + 1) * n_spread // n_slots)
            project(due - (n_spread - len(pieces)))
        state[head] = st
    assert not pieces


def _hgrn_chunk(qx, z, vx, gx, st, lb, gain, level, odd_row):
    n_levels = int(math.log2(CHUNK))
    q = qx * jax.nn.sigmoid(qx) * (HGRN_HEAD_DIM ** -0.5)
    v_t = vx.T.astype(BF16)
    t = jnp.exp2(_neg_abs(z) * LOG2_E)
    r = 1.0 / (1.0 + t)
    sig_pos = jnp.where(z >= 0, r, t * r)
    sig_neg = jnp.where(z >= 0, t * r, r)
    f = lb + (1.0 - lb) * sig_pos
    k = (1.0 - lb) * sig_neg
    g = _chunk_cumsum(jnp.log(f) * LOG2_E)
    g_last = g[CHUNK - 1:CHUNK, :]

    q_bf, k_bf = q.astype(BF16), k.astype(BF16)
    a = _dot_nt(q_bf, k_bf)
    a = jnp.where(level == n_levels, a, 0.0)
    for lv in range(n_levels):
        if lv == 0:
            x = jnp.where(odd_row, f, 1.0)
        else:
            x = jnp.exp2(_neg_abs(g - _level_anchor(g, 1 << lv)))
        x = x.astype(BF16)
        a = jnp.where(level == lv, _dot_nt(q_bf * x, k_bf * x), a)

    lhs = jnp.concatenate([a.astype(BF16), (q * jnp.exp2(g)).astype(BF16)], axis=1)
    rhs_t = jnp.concatenate([v_t, st.astype(BF16)], axis=1)
    o = _dot_nt(lhs, rhs_t)
    k_out = (k * jnp.exp2(g_last - g)).astype(BF16)
    st = st * jnp.exp2(g_last) + jnp.dot(v_t, k_out, preferred_element_type=F32)
    return _rms(o, gain) * (gx * jax.nn.sigmoid(gx)), st


def _in_proj_hgrn(x2d, norm_gain, w_bf16, lb_logits, out_gain, level_map, layer, seq):
    t, d = x2d.shape
    n = w_bf16.shape[1]
    full = lambda a: pl.BlockSpec(a.shape, lambda i: (0,) * a.ndim)
    row = lambda width: pl.BlockSpec((ROW_TILE, width), lambda i: (i, 0))
    consts = (norm_gain.reshape(1, d), w_bf16, lb_logits, out_gain.reshape(1, LANES), level_map)
    return pl.pallas_call(
        functools.partial(_in_proj_hgrn_kernel, layer=layer, tiles_per_seq=seq // ROW_TILE),
        grid=(t // ROW_TILE,),
        in_specs=[row(d)] + [full(a) for a in consts],
        out_specs=[row(3 * ATTN_WIDTH), row(HGRN_WIDTH)],
        out_shape=[jax.ShapeDtypeStruct((t, 3 * ATTN_WIDTH), F32),
                   jax.ShapeDtypeStruct((t, HGRN_WIDTH), F32)],
        scratch_shapes=[pltpu.VMEM((ROW_TILE, n - 3 * ATTN_WIDTH), F32),
                        pltpu.VMEM((HGRN_HEADS, HGRN_HEAD_DIM, HGRN_HEAD_DIM), F32)],
        compiler_params=pltpu.CompilerParams(
            dimension_semantics=("arbitrary",), vmem_limit_bytes=V7X_VMEM_LIMIT),
        name="in_proj_hgrn",
    )(x2d, *consts)


def _intra_chunk_levels():
    i = jnp.arange(CHUNK, dtype=jnp.int32)[:, None]
    j = jnp.arange(CHUNK, dtype=jnp.int32)[None, :]
    x = i ^ j
    lv = jnp.zeros((CHUNK, CHUNK), jnp.int32)
    for bit in range(1, int(math.log2(CHUNK))):
        lv = jnp.where(x >= (1 << bit), bit, lv)
    lv = jnp.where(i == j, int(math.log2(CHUNK)), lv)
    return jnp.where(j > i, -1, lv)


def _mix_mlp_kernel(x_ref, attn_ref, rec_ref, ag_ref, wo_ref, nm_ref, wu_ref, wd_ref,
                    nf_ref, o_ref, *, final_norm):
    attn = _rms(attn_ref[...], ag_ref[...]).astype(BF16)
    x1 = (x_ref[...]
          + jnp.dot(attn, wo_ref[0:ATTN_WIDTH, :], preferred_element_type=F32)
          + jnp.dot(rec_ref[...].astype(BF16), wo_ref[ATTN_WIDTH:MIX_WIDTH, :],
                    preferred_element_type=F32))
    h = _rms(x1, nm_ref[...]).astype(BF16)
    o_ref[...] = x1
    for c0 in range(0, MLP_HIDDEN, COL_CHUNK):
        u = jnp.dot(h, wu_ref[:, c0:c0 + COL_CHUNK], preferred_element_type=F32)
        u = jnp.square(jnp.maximum(u, 0.0)).astype(BF16)
        o_ref[...] += jnp.dot(u, wd_ref[c0:c0 + COL_CHUNK, :], preferred_element_type=F32)
    if final_norm:
        o_ref[...] = _rms(o_ref[...], nf_ref[...])


def _mix_mlp(x2d, attn2d, rec2d, attn_gain, w_out, norm_mlp, w_up, w_down, norm_final,
             final_norm):
    t, d = x2d.shape
    row = lambda w: pl.BlockSpec((ROW_TILE, w), lambda i: (i, 0))
    full = lambda a: pl.BlockSpec(a.shape, lambda i: (0, 0))
    vec = lambda a: a.reshape(1, -1)
    args = (x2d, attn2d, rec2d, vec(attn_gain), w_out, vec(norm_mlp), w_up, w_down,
            vec(norm_final))
    return pl.pallas_call(
        functools.partial(_mix_mlp_kernel, final_norm=final_norm),
        grid=(t // ROW_TILE,),
        in_specs=[row(d), row(ATTN_WIDTH), row(HGRN_WIDTH)] + [full(a) for a in args[3:]],
        out_specs=row(d),
        out_shape=jax.ShapeDtypeStruct((t, d), F32),
        compiler_params=pltpu.CompilerParams(
            dimension_semantics=("arbitrary",), vmem_limit_bytes=V7X_VMEM_LIMIT),
        name="mix_mlp",
    )(*args)


def _rotary_tables(seq):
    half = ATTN_HEAD_DIM // 2
    inv_freq = ROPE_THETA ** (-jnp.arange(half, dtype=F32) / half)
    ang = jnp.arange(seq, dtype=F32)[:, None] * inv_freq[None, :]
    cos, sin = jnp.cos(ang), jnp.sin(ang)
    return (jnp.concatenate([cos, cos, cos, cos], axis=1),
            jnp.concatenate([-sin, -sin, sin, sin], axis=1))


def _arrange_in_proj_columns(w_in_layer):
    d = w_in_layer.shape[0]
    half = ATTN_HEAD_DIM // 2
    qk = w_in_layer[:, :2 * ATTN_WIDTH].reshape(d, 2 * ATTN_WIDTH // LANES, 2, 2, half)
    qk = qk.transpose(0, 1, 3, 2, 4).reshape(d, 2 * ATTN_WIDTH)
    v = w_in_layer[:, 2 * ATTN_WIDTH:3 * ATTN_WIDTH]
    rec = w_in_layer[:, 3 * ATTN_WIDTH:].reshape(d, 4, HGRN_HEADS, HGRN_HEAD_DIM)
    rec = rec.transpose(0, 2, 1, 3).reshape(d, 4 * HGRN_WIDTH)
    return jnp.concatenate([rec, qk, v], axis=1)


def kernel(x, norm_mix, w_in, attn_out_gain, hgrn_lb_logits, hgrn_out_gain, w_out, norm_mlp,
           w_up, w_down, norm_final):
    b, seq, d = x.shape
    depth = w_in.shape[0]
    cos, sin = _rotary_tables(seq)
    level_map = _intra_chunk_levels()
    x2d = x.reshape(b * seq, d)
    for layer in range(depth):
        qkv, rec = _in_proj_hgrn(x2d, norm_mix[layer],
                                 _arrange_in_proj_columns(w_in[layer]).astype(BF16),
                                 hgrn_lb_logits, hgrn_out_gain[layer], level_map, layer, seq)
        attn = _dilated_attention(qkv.reshape(b, seq, 3 * ATTN_WIDTH), cos, sin)
        x2d = _mix_mlp(x2d, attn.reshape(b * seq, ATTN_WIDTH), rec,
                       attn_out_gain[layer], w_out[layer].astype(BF16), norm_mlp[layer],
                       w_up[layer].astype(BF16), w_down[layer].astype(BF16), norm_final,
                       final_norm=(layer == depth - 1))
    return x2d.reshape(b, seq, d)
```

```python
import functools
import math

import jax
import jax.numpy as jnp
from jax import lax
from jax.experimental import pallas as pl
from jax.experimental.pallas import tpu as pltpu

F32 = jnp.float32
BF16 = jnp.bfloat16

D_MODEL = 1024
ATTN_HEADS = 8
ATTN_HEAD_DIM = 64
ATTN_WIDTH = ATTN_HEADS * ATTN_HEAD_DIM
HGRN_HEADS = 4
HGRN_HEAD_DIM = 128
HGRN_WIDTH = HGRN_HEADS * HGRN_HEAD_DIM
MIX_WIDTH = ATTN_WIDTH + HGRN_WIDTH
IN_PROJ_WIDTH = 3 * ATTN_WIDTH + 4 * HGRN_WIDTH
DILATIONS = (1, 4, 16)
SPAN = 128
ROPE_THETA = 10000.0
MLP_HIDDEN = 4 * D_MODEL
NORM_EPS = 1e-6
MASK_VALUE = -1e30

LANES = 128
V7X_VMEM_LIMIT = 56 * 1024 * 1024
ROW_TILE = 512
PROJ_ROW_TILE = 512
COL_CHUNK = 512
CHUNK = 128
ATTN_UNROLL = 4
PROJ_PIECE = 256
LOG2_E = 1.4426950408889634


def _rms(x, gain):
    return x * lax.rsqrt(jnp.mean(x * x, axis=-1, keepdims=True) + NORM_EPS) * gain


def _attn_kernel(q_ref, k_ref, v_ref, o_ref, out_a, out_b, lse_a, lse_b, p_buf, m_buf,
                 *, seq):
    lane = lax.broadcasted_iota(jnp.int32, (SPAN, LANES), 1)
    head0 = lane < ATTN_HEAD_DIM
    qk_head0 = (lane & (ATTN_HEAD_DIM // 2)) == 0
    q_minus_k = (lax.broadcasted_iota(jnp.int32, (SPAN, SPAN), 0)
                 - lax.broadcasted_iota(jnp.int32, (SPAN, SPAN), 1))
    prev_ok = q_minus_k <= 0
    cur_ok = q_minus_k >= 0
    ones = jnp.ones((SPAN, LANES), BF16)
    n_tiles = seq // SPAN

    assert DILATIONS[0] == 1 and len(DILATIONS) == 3
    parked = {DILATIONS[1]: (out_a, lse_a), DILATIONS[2]: (out_b, lse_b)}

    for dil in sorted(DILATIONS, reverse=True):
        n_blk = n_tiles // dil
        tiles = [(r, n) for r in range(dil) for n in range(n_blk)]

        def block_rows(r, n, dil=dil):
            start = r + dil * SPAN * n
            return pl.ds(start, SPAN) if dil == 1 else pl.ds(start, SPAN, stride=dil)

        def probs(group, block_rows=block_rows):
            key_blocks = {}

            def keys(r, n):
                if (r, n) not in key_blocks:
                    key_blocks[r, n] = k_ref[block_rows(r, n), :].astype(BF16)
                return key_blocks[r, n]

            for slot, (r, n) in enumerate(group):
                qt = q_ref[block_rows(r, n), :]
                kt = keys(r, n) if n == 0 else jnp.concatenate([keys(r, n - 1), keys(r, n)], 0)
                scores = []
                for head in range(2):
                    qh = jnp.where(qk_head0 if head == 0 else ~qk_head0, qt, 0.0).astype(BF16)
                    scores.append(_dot_nt(qh, kt))
                mx = []
                for head, s in enumerate(scores):
                    s_cur = jnp.where(cur_ok, s[:, -SPAN:], MASK_VALUE)
                    if n == 0:
                        m = jnp.max(s_cur, axis=-1, keepdims=True)
                    else:
                        s_prev = jnp.where(prev_ok, s[:, :SPAN], MASK_VALUE)
                        m = jnp.max(jnp.maximum(s_prev, s_cur), axis=-1, keepdims=True)
                        p_buf[slot, head, :, 0:SPAN] = jnp.exp2(s_prev - m).astype(BF16)
                    p_buf[slot, head, :, SPAN:2 * SPAN] = jnp.exp2(s_cur - m).astype(BF16)
                    mx.append(jnp.broadcast_to(m, (SPAN, LANES)))
                m_buf[slot] = jnp.where(head0, mx[0], mx[1])

        def weighted_values(group, block_rows=block_rows, dil=dil):
            value_blocks = {}

            def values(r, n):
                if (r, n) not in value_blocks:
                    value_blocks[r, n] = jnp.concatenate(
                        [v_ref[block_rows(r, n), :].astype(BF16), ones], axis=1)
                return value_blocks[r, n]

            for slot, (r, n) in enumerate(group):
                if n == 0:
                    vt = values(r, n)
                    res = [jnp.dot(p_buf[slot, head, :, SPAN:2 * SPAN], vt,
                                   preferred_element_type=F32) for head in range(2)]
                else:
                    vt = jnp.concatenate([values(r, n - 1), values(r, n)], axis=0)
                    res = [jnp.dot(p_buf[slot, head], vt, preferred_element_type=F32)
                           for head in range(2)]
                denom = jnp.where(head0, res[0][:, LANES:], res[1][:, LANES:])
                out = jnp.where(head0, res[0][:, :LANES], res[1][:, :LANES]) / denom
                lse = m_buf[slot] + jnp.log(denom) * LOG2_E
                q_rows = block_rows(r, n)
                if dil in parked:
                    parked[dil][0][q_rows, :] = out
                    parked[dil][1][q_rows, :] = lse
                else:
                    a, b = lse_a[q_rows, :], lse_b[q_rows, :]
                    top = jnp.maximum(jnp.maximum(lse, a), b)
                    w, wa, wb = jnp.exp2(lse - top), jnp.exp2(a - top), jnp.exp2(b - top)
                    o_ref[q_rows, :] = ((w * out + wa * out_a[q_rows, :] + wb * out_b[q_rows, :])
                                        / (w + wa + wb))

        groups = [tiles[i:i + ATTN_UNROLL] for i in range(0, len(tiles), ATTN_UNROLL)]
        probs(groups[0])
        for done, ahead in zip(groups[:-1], groups[1:]):
            weighted_values(done)
            probs(ahead)
        weighted_values(groups[-1])


def _dilated_attention(qkv3d):
    b, seq, _ = qkv3d.shape
    n_pairs = ATTN_WIDTH // LANES
    blk = lambda off: pl.BlockSpec((None, seq, LANES), lambda i, j: (i, 0, off + j))
    slab = pltpu.VMEM((seq, LANES), F32)
    return pl.pallas_call(
        functools.partial(_attn_kernel, seq=seq),
        grid=(b, n_pairs),
        in_specs=[blk(0), blk(n_pairs), blk(2 * n_pairs)],
        out_specs=pl.BlockSpec((None, seq, LANES), lambda i, j: (i, 0, j)),
        out_shape=jax.ShapeDtypeStruct((b, seq, ATTN_WIDTH), F32),
        scratch_shapes=[slab] * 4 + [
            pltpu.VMEM((ATTN_UNROLL, 2, SPAN, 2 * SPAN), BF16),
            pltpu.VMEM((ATTN_UNROLL, SPAN, LANES), F32)],
        compiler_params=pltpu.CompilerParams(
            dimension_semantics=("arbitrary", "arbitrary"),
            vmem_limit_bytes=V7X_VMEM_LIMIT),
        name="dilated_attn",
    )(qkv3d, qkv3d, qkv3d)


def _level_anchor(g, half):
    rows = g.shape[0]
    if 2 * half >= 8:
        g3 = g.reshape(rows // (2 * half), 2 * half, LANES)
        return jnp.broadcast_to(g3[:, half - 1:half, :], g3.shape).reshape(rows, LANES)
    g3 = g.reshape(rows // 8, 8, LANES)
    sub = lax.broadcasted_iota(jnp.int32, g3.shape, 1)
    out = jnp.broadcast_to(g3[:, half - 1:half, :], g3.shape)
    for first in range(2 * half, 8, 2 * half):
        src = jnp.broadcast_to(g3[:, first + half - 1:first + half, :], g3.shape)
        out = jnp.where(sub >= first, src, out)
    return out.reshape(rows, LANES)


def _neg_abs(x):
    return -jnp.abs(x)


def _dot_nt(a, b):
    return lax.dot_general(a, b, (((1,), (1,)), ((), ())), preferred_element_type=F32)


def _chunk_cumsum(x):
    row = lax.broadcasted_iota(jnp.int32, x.shape, 0)
    shift = 1
    while shift < x.shape[0]:
        x = x + jnp.where(row >= shift, pltpu.roll(x, shift, axis=0), 0.0)
        shift *= 2
    return x


def _in_proj_hgrn_kernel(x_ref, cq_ref, sq_ref, ck_ref, sk_ref, nm_ref, w_ref, lbl_ref,
                         gain_ref, lvl_ref, qkv_ref, rec_ref, hin, state,
                         *, layer, tiles_per_seq):
    h = _rms(x_ref[...], nm_ref[...]).astype(BF16)
    head_cols = 4 * HGRN_HEAD_DIM
    restart = pl.program_id(0) % tiles_per_seq == 0

    logits = lbl_ref[...]
    e = jnp.exp(logits - jnp.max(logits, axis=0, keepdims=True))
    share = e / jnp.sum(e, axis=0, keepdims=True)
    lb_all = jnp.sum(share[0:layer + 1, :], axis=0, keepdims=True) - share[0:1, :]

    n_rec = HGRN_HEADS * head_cols
    pieces = [(hin, c, c, None) for c in range(0, n_rec, PROJ_PIECE)]
    for c in range(0, 3 * ATTN_WIDTH, PROJ_PIECE):
        tables = ((cq_ref, sq_ref), (ck_ref, sk_ref), None)[c // ATTN_WIDTH]
        pieces.append((qkv_ref, c, n_rec + c, tables))

    def project(n_pieces):
        for _ in range(n_pieces):
            dst, c_dst, c_w, tables = pieces.pop(0)
            y = jnp.dot(h, w_ref[:, c_w:c_w + PROJ_PIECE], preferred_element_type=F32)
            if tables is None:
                dst[:, c_dst:c_dst + PROJ_PIECE] = y
                continue
            cos, sin = tables[0][...], tables[1][...]
            for b0 in range(0, PROJ_PIECE, LANES):
                yb = y[:, b0:b0 + LANES]
                dst[:, c_dst + b0:c_dst + b0 + LANES] = (
                    yb * cos + pltpu.roll(yb, LANES // 2, axis=1) * sin)

    project(head_cols // PROJ_PIECE)
    n_spread = len(pieces)
    chunks_per_head = x_ref.shape[0] // CHUNK
    n_slots = HGRN_HEADS * chunks_per_head
    level = lvl_ref[...]
    odd_row = (lax.broadcasted_iota(jnp.int32, (CHUNK, LANES), 0) & 1) == 1
    for head in range(HGRN_HEADS):
        c0 = head * head_cols
        lb = lb_all[:, head * HGRN_HEAD_DIM:(head + 1) * HGRN_HEAD_DIM]
        st = jnp.where(restart, 0.0, state[head])
        for c in range(chunks_per_head):
            r0 = c * CHUNK
            cols = lambda j: hin[r0:r0 + CHUNK, c0 + j * LANES:c0 + (j + 1) * LANES]
            out, st = _hgrn_chunk(cols(0), cols(1), cols(2), cols(3), st, lb,
                                  gain_ref[...], level, odd_row)
            rec_ref[r0:r0 + CHUNK, head * HGRN_HEAD_DIM:(head + 1) * HGRN_HEAD_DIM] = out
            slot = head * chunks_per_head + c
            due = -(-(slot + 1) * n_spread // n_slots)
            project(due - (n_spread - len(pieces)))
        state[head] = st
    assert not pieces


def _hgrn_chunk(qx, z, vx, gx, st, lb, gain, level, odd_row):
    n_levels = int(math.log2(CHUNK))
    q = qx * jax.nn.sigmoid(qx) * (HGRN_HEAD_DIM ** -0.5)
    v_t = vx.T.astype(BF16)
    t = jnp.exp2(_neg_abs(z) * LOG2_E)
    r = 1.0 / (1.0 + t)
    sig_pos = jnp.where(z >= 0, r, t * r)
    sig_neg = jnp.where(z >= 0, t * r, r)
    f = lb + (1.0 - lb) * sig_pos
    k = (1.0 - lb) * sig_neg
    g = _chunk_cumsum(jnp.log(f) * LOG2_E)
    g_last = g[CHUNK - 1:CHUNK, :]

    q_bf, k_bf = q.astype(BF16), k.astype(BF16)
    a = _dot_nt(q_bf, k_bf)
    a = jnp.where(level == n_levels, a, 0.0)
    for lv in range(n_levels):
        if lv == 0:
            x = jnp.where(odd_row, f, 1.0)
        else:
            x = jnp.exp2(_neg_abs(g - _level_anchor(g, 1 << lv)))
        x = x.astype(BF16)
        a = jnp.where(level == lv, _dot_nt(q_bf * x, k_bf * x), a)

    lhs = jnp.concatenate([a.astype(BF16), (q * jnp.exp2(g)).astype(BF16)], axis=1)
    rhs_t = jnp.concatenate([v_t, st.astype(BF16)], axis=1)
    o = _dot_nt(lhs, rhs_t)
    k_out = (k * jnp.exp2(g_last - g)).astype(BF16)
    st = st * jnp.exp2(g_last) + jnp.dot(v_t, k_out, preferred_element_type=F32)
    return _rms(o, gain) * (gx * jax.nn.sigmoid(gx)), st


def _in_proj_hgrn(x2d, rotary, norm_gain, w_bf16, lb_logits, out_gain, level_map, layer, seq):
    t, d = x2d.shape
    n = w_bf16.shape[1]
    tiles_per_seq = seq // PROJ_ROW_TILE
    full = lambda a: pl.BlockSpec(a.shape, lambda i: (0,) * a.ndim)
    row = lambda width: pl.BlockSpec((PROJ_ROW_TILE, width), lambda i: (i, 0))
    pos = pl.BlockSpec((PROJ_ROW_TILE, LANES), lambda i: (i % tiles_per_seq, 0))
    consts = (norm_gain.reshape(1, d), w_bf16, lb_logits, out_gain.reshape(1, LANES), level_map)
    return pl.pallas_call(
        functools.partial(_in_proj_hgrn_kernel, layer=layer, tiles_per_seq=tiles_per_seq),
        grid=(t // PROJ_ROW_TILE,),
        in_specs=[row(d)] + [pos] * len(rotary) + [full(a) for a in consts],
        out_specs=[row(3 * ATTN_WIDTH), row(HGRN_WIDTH)],
        out_shape=[jax.ShapeDtypeStruct((t, 3 * ATTN_WIDTH), F32),
                   jax.ShapeDtypeStruct((t, HGRN_WIDTH), F32)],
        scratch_shapes=[pltpu.VMEM((PROJ_ROW_TILE, n - 3 * ATTN_WIDTH), F32),
                        pltpu.VMEM((HGRN_HEADS, HGRN_HEAD_DIM, HGRN_HEAD_DIM), F32)],
        compiler_params=pltpu.CompilerParams(
            dimension_semantics=("arbitrary",), vmem_limit_bytes=V7X_VMEM_LIMIT),
        name="in_proj_hgrn",
    )(x2d, *rotary, *consts)


def _intra_chunk_levels():
    i = jnp.arange(CHUNK, dtype=jnp.int32)[:, None]
    j = jnp.arange(CHUNK, dtype=jnp.int32)[None, :]
    x = i ^ j
    lv = jnp.zeros((CHUNK, CHUNK), jnp.int32)
    for bit in range(1, int(math.log2(CHUNK))):
        lv = jnp.where(x >= (1 << bit), bit, lv)
    lv = jnp.where(i == j, int(math.log2(CHUNK)), lv)
    return jnp.where(j > i, -1, lv)


def _mix_mlp_kernel(x_ref, attn_ref, rec_ref, ag_ref, wo_ref, nm_ref, wu_ref, wd_ref,
                    nf_ref, o_ref, *, final_norm):
    attn = _rms(attn_ref[...], ag_ref[...]).astype(BF16)
    x1 = (x_ref[...]
          + jnp.dot(attn, wo_ref[0:ATTN_WIDTH, :], preferred_element_type=F32)
          + jnp.dot(rec_ref[...].astype(BF16), wo_ref[ATTN_WIDTH:MIX_WIDTH, :],
                    preferred_element_type=F32))
    h = _rms(x1, nm_ref[...]).astype(BF16)
    o_ref[...] = x1
    for c0 in range(0, MLP_HIDDEN, COL_CHUNK):
        u = jnp.dot(h, wu_ref[:, c0:c0 + COL_CHUNK], preferred_element_type=F32)
        u = jnp.square(jnp.maximum(u, 0.0)).astype(BF16)
        o_ref[...] += jnp.dot(u, wd_ref[c0:c0 + COL_CHUNK, :], preferred_element_type=F32)
    if final_norm:
        o_ref[...] = _rms(o_ref[...], nf_ref[...])


def _mix_mlp(x2d, attn2d, rec2d, attn_gain, w_out, norm_mlp, w_up, w_down, norm_final,
             final_norm):
    t, d = x2d.shape
    row = lambda w: pl.BlockSpec((ROW_TILE, w), lambda i: (i, 0))
    full = lambda a: pl.BlockSpec(a.shape, lambda i: (0, 0))
    vec = lambda a: a.reshape(1, -1)
    args = (x2d, attn2d, rec2d, vec(attn_gain), w_out, vec(norm_mlp), w_up, w_down,
            vec(norm_final))
    return pl.pallas_call(
        functools.partial(_mix_mlp_kernel, final_norm=final_norm),
        grid=(t // ROW_TILE,),
        in_specs=[row(d), row(ATTN_WIDTH), row(HGRN_WIDTH)] + [full(a) for a in args[3:]],
        out_specs=row(d),
        out_shape=jax.ShapeDtypeStruct((t, d), F32),
        compiler_params=pltpu.CompilerParams(
            dimension_semantics=("arbitrary",), vmem_limit_bytes=V7X_VMEM_LIMIT),
        name="mix_mlp",
    )(*args)


def _rotary_tables(seq):
    half = ATTN_HEAD_DIM // 2
    inv_freq = ROPE_THETA ** (-jnp.arange(half, dtype=F32) / half)
    ang = jnp.arange(seq, dtype=F32)[:, None] * inv_freq[None, :]
    cos, sin = jnp.cos(ang), jnp.sin(ang)
    cos = jnp.concatenate([cos, cos, cos, cos], axis=1)
    sin = jnp.concatenate([-sin, -sin, sin, sin], axis=1)
    q_scale = ATTN_HEAD_DIM ** -0.5 * LOG2_E
    return cos * q_scale, sin * q_scale, cos, sin


def _arrange_in_proj_columns(w_in_layer):
    d = w_in_layer.shape[0]
    half = ATTN_HEAD_DIM // 2
    qk = w_in_layer[:, :2 * ATTN_WIDTH].reshape(d, 2 * ATTN_WIDTH // LANES, 2, 2, half)
    qk = qk.transpose(0, 1, 3, 2, 4).reshape(d, 2 * ATTN_WIDTH)
    v = w_in_layer[:, 2 * ATTN_WIDTH:3 * ATTN_WIDTH]
    rec = w_in_layer[:, 3 * ATTN_WIDTH:].reshape(d, 4, HGRN_HEADS, HGRN_HEAD_DIM)
    rec = rec.transpose(0, 2, 1, 3).reshape(d, 4 * HGRN_WIDTH)
    return jnp.concatenate([rec, qk, v], axis=1)


def kernel(x, norm_mix, w_in, attn_out_gain, hgrn_lb_logits, hgrn_out_gain, w_out, norm_mlp,
           w_up, w_down, norm_final):
    b, seq, d = x.shape
    depth = w_in.shape[0]
    rotary = _rotary_tables(seq)
    level_map = _intra_chunk_levels()
    x2d = x.reshape(b * seq, d)
    for layer in range(depth):
        qkv, rec = _in_proj_hgrn(x2d, rotary, norm_mix[layer],
                                 _arrange_in_proj_columns(w_in[layer]).astype(BF16),
                                 hgrn_lb_logits, hgrn_out_gain[layer], level_map, layer, seq)
        attn = _dilated_attention(qkv.reshape(b, seq, 3 * ATTN_WIDTH))
        x2d = _mix_mlp(x2d, attn.reshape(b * seq, ATTN_WIDTH), rec,
                       attn_out_gain[layer], w_out[layer].astype(BF16), norm_mlp[layer],
                       w_up[layer].astype(BF16), w_down[layer].astype(BF16), norm_final,
                       final_norm=(layer == depth - 1))
    return x2d.reshape(b, seq, d)
```

```python
import functools
import math

import jax
import jax.numpy as jnp
import numpy as np
from jax import lax
from jax.experimental import pallas as pl
from jax.experimental.pallas import tpu as pltpu

F32 = jnp.float32
BF16 = jnp.bfloat16

D_MODEL = 1024
ATTN_HEADS = 8
ATTN_HEAD_DIM = 64
ATTN_WIDTH = ATTN_HEADS * ATTN_HEAD_DIM
HGRN_HEADS = 4
HGRN_HEAD_DIM = 128
HGRN_WIDTH = HGRN_HEADS * HGRN_HEAD_DIM
MIX_WIDTH = ATTN_WIDTH + HGRN_WIDTH
DILATIONS = (1, 4, 16)
SPAN = 128
ROPE_THETA = 10000.0
MLP_HIDDEN = 4 * D_MODEL
NORM_EPS = 1e-6
MASK_VALUE = -1e30

LANES = 128
V7X_VMEM_LIMIT = 56 * 1024 * 1024
ROW_TILE = 512
COL_CHUNK = 512
CHUNK = 128
ATTN_UNROLL = 4
PROJ_PIECE = 256
LOG2_E = 1.4426950408889634


def _rms(x, gain):
    return x * lax.rsqrt(jnp.mean(x * x, axis=-1, keepdims=True) + NORM_EPS) * gain


def _dot_nt(a, b):
    return lax.dot_general(a, b, (((1,), (1,)), ((), ())), preferred_element_type=F32)


def _attn_kernel(q_ref, k_ref, v_ref, o_ref, out_a, out_b, lse_a, lse_b, p_buf, m_buf,
                 *, seq):
    lane = lax.broadcasted_iota(jnp.int32, (SPAN, LANES), 1)
    head0 = lane < ATTN_HEAD_DIM
    qk_head0 = (lane & (ATTN_HEAD_DIM // 2)) == 0
    q_minus_k = (lax.broadcasted_iota(jnp.int32, (SPAN, SPAN), 0)
                 - lax.broadcasted_iota(jnp.int32, (SPAN, SPAN), 1))
    prev_ok = q_minus_k <= 0
    cur_ok = q_minus_k >= 0
    ones = jnp.ones((SPAN, LANES), BF16)
    n_tiles = seq // SPAN

    assert DILATIONS[0] == 1 and len(DILATIONS) == 3
    parked = {DILATIONS[1]: (out_a, lse_a), DILATIONS[2]: (out_b, lse_b)}

    for dil in sorted(DILATIONS, reverse=True):
        n_blk = n_tiles // dil
        tiles = [(r, n) for r in range(dil) for n in range(n_blk)]

        def block_rows(r, n, dil=dil):
            start = r + dil * SPAN * n
            return pl.ds(start, SPAN) if dil == 1 else pl.ds(start, SPAN, stride=dil)

        def probs(group, block_rows=block_rows):
            key_blocks = {}

            def keys(r, n):
                if (r, n) not in key_blocks:
                    key_blocks[r, n] = k_ref[block_rows(r, n), :].astype(BF16)
                return key_blocks[r, n]

            for slot, (r, n) in enumerate(group):
                qt = q_ref[block_rows(r, n), :]
                kt = keys(r, n) if n == 0 else jnp.concatenate([keys(r, n - 1), keys(r, n)], 0)
                scores = []
                for head in range(2):
                    qh = jnp.where(qk_head0 if head == 0 else ~qk_head0, qt, 0.0).astype(BF16)
                    scores.append(_dot_nt(qh, kt))
                mx = []
                for head, s in enumerate(scores):
                    s_cur = jnp.where(cur_ok, s[:, -SPAN:], MASK_VALUE)
                    if n == 0:
                        m = jnp.max(s_cur, axis=-1, keepdims=True)
                    else:
                        s_prev = jnp.where(prev_ok, s[:, :SPAN], MASK_VALUE)
                        m = jnp.max(jnp.maximum(s_prev, s_cur), axis=-1, keepdims=True)
                        p_buf[slot, head, :, 0:SPAN] = jnp.exp2(s_prev - m).astype(BF16)
                    p_buf[slot, head, :, SPAN:2 * SPAN] = jnp.exp2(s_cur - m).astype(BF16)
                    mx.append(jnp.broadcast_to(m, (SPAN, LANES)))
                m_buf[slot] = jnp.where(head0, mx[0], mx[1])

        def weighted_values(group, block_rows=block_rows, dil=dil):
            value_blocks = {}

            def values(r, n):
                if (r, n) not in value_blocks:
                    value_blocks[r, n] = jnp.concatenate(
                        [v_ref[block_rows(r, n), :].astype(BF16), ones], axis=1)
                return value_blocks[r, n]

            for slot, (r, n) in enumerate(group):
                if n == 0:
                    vt = values(r, n)
                    res = [jnp.dot(p_buf[slot, head, :, SPAN:2 * SPAN], vt,
                                   preferred_element_type=F32) for head in range(2)]
                else:
                    vt = jnp.concatenate([values(r, n - 1), values(r, n)], axis=0)
                    res = [jnp.dot(p_buf[slot, head], vt, preferred_element_type=F32)
                           for head in range(2)]
                denom = jnp.where(head0, res[0][:, LANES:], res[1][:, LANES:])
                out = jnp.where(head0, res[0][:, :LANES], res[1][:, :LANES]) / denom
                lse = m_buf[slot] + jnp.log(denom) * LOG2_E
                q_rows = block_rows(r, n)
                if dil in parked:
                    parked[dil][0][q_rows, :] = out
                    parked[dil][1][q_rows, :] = lse
                else:
                    a, b = lse_a[q_rows, :], lse_b[q_rows, :]
                    top = jnp.maximum(jnp.maximum(lse, a), b)
                    w, wa, wb = jnp.exp2(lse - top), jnp.exp2(a - top), jnp.exp2(b - top)
                    o_ref[q_rows, :] = ((w * out + wa * out_a[q_rows, :] + wb * out_b[q_rows, :])
                                        / (w + wa + wb))

        groups = [tiles[i:i + ATTN_UNROLL] for i in range(0, len(tiles), ATTN_UNROLL)]
        probs(groups[0])
        for done, ahead in zip(groups[:-1], groups[1:]):
            weighted_values(done)
            probs(ahead)
        weighted_values(groups[-1])


def _dilated_attention(qkv3d):
    b, seq, _ = qkv3d.shape
    n_pairs = ATTN_WIDTH // LANES
    blk = lambda off: pl.BlockSpec((None, seq, LANES), lambda i, j: (i, 0, off + j))
    slab = pltpu.VMEM((seq, LANES), F32)
    return pl.pallas_call(
        functools.partial(_attn_kernel, seq=seq),
        grid=(b, n_pairs),
        in_specs=[blk(0), blk(n_pairs), blk(2 * n_pairs)],
        out_specs=pl.BlockSpec((None, seq, LANES), lambda i, j: (i, 0, j)),
        out_shape=jax.ShapeDtypeStruct((b, seq, ATTN_WIDTH), F32),
        scratch_shapes=[slab] * 4 + [
            pltpu.VMEM((ATTN_UNROLL, 2, SPAN, 2 * SPAN), BF16),
            pltpu.VMEM((ATTN_UNROLL, SPAN, LANES), F32)],
        compiler_params=pltpu.CompilerParams(
            dimension_semantics=("arbitrary", "arbitrary"),
            vmem_limit_bytes=V7X_VMEM_LIMIT),
        name="dilated_attn",
    )(qkv3d, qkv3d, qkv3d)


def _level_anchor(g, half):
    rows = g.shape[0]
    if 2 * half >= 8:
        g3 = g.reshape(rows // (2 * half), 2 * half, LANES)
        return jnp.broadcast_to(g3[:, half - 1:half, :], g3.shape).reshape(rows, LANES)
    g3 = g.reshape(rows // 8, 8, LANES)
    sub = lax.broadcasted_iota(jnp.int32, g3.shape, 1)
    out = jnp.broadcast_to(g3[:, half - 1:half, :], g3.shape)
    for first in range(2 * half, 8, 2 * half):
        src = jnp.broadcast_to(g3[:, first + half - 1:first + half, :], g3.shape)
        out = jnp.where(sub >= first, src, out)
    return out.reshape(rows, LANES)


def _neg_abs(x):
    return -jnp.abs(x)


def _chunk_cumsum(x):
    row = lax.broadcasted_iota(jnp.int32, x.shape, 0)
    shift = 1
    while shift < x.shape[0]:
        x = x + jnp.where(row >= shift, pltpu.roll(x, shift, axis=0), 0.0)
        shift *= 2
    return x


def _in_proj_hgrn_kernel(x_ref, cq_ref, sq_ref, ck_ref, sk_ref, nm_ref, w_ref, lbl_ref,
                         gain_ref, lvl_ref, qkv_ref, rec_ref, hin, state,
                         *, layer, tiles_per_seq):
    h = _rms(x_ref[...], nm_ref[...]).astype(BF16)
    head_cols = 4 * HGRN_HEAD_DIM
    restart = pl.program_id(0) % tiles_per_seq == 0

    logits = lbl_ref[...]
    e = jnp.exp(logits - jnp.max(logits, axis=0, keepdims=True))
    share = e / jnp.sum(e, axis=0, keepdims=True)
    lb_all = jnp.sum(share[0:layer + 1, :], axis=0, keepdims=True) - share[0:1, :]

    n_rec = HGRN_HEADS * head_cols
    pieces = [(hin, c, c, None) for c in range(0, n_rec, PROJ_PIECE)]
    for c in range(0, 3 * ATTN_WIDTH, PROJ_PIECE):
        tables = ((cq_ref, sq_ref), (ck_ref, sk_ref), None)[c // ATTN_WIDTH]
        pieces.append((qkv_ref, c, n_rec + c, tables))

    def project(n_pieces):
        for _ in range(n_pieces):
            dst, c_dst, c_w, tables = pieces.pop(0)
            y = jnp.dot(h, w_ref[:, c_w:c_w + PROJ_PIECE], preferred_element_type=F32)
            if tables is None:
                dst[:, c_dst:c_dst + PROJ_PIECE] = y
                continue
            cos, sin = tables[0][...], tables[1][...]
            for b0 in range(0, PROJ_PIECE, LANES):
                yb = y[:, b0:b0 + LANES]
                dst[:, c_dst + b0:c_dst + b0 + LANES] = (
                    yb * cos + pltpu.roll(yb, LANES // 2, axis=1) * sin)

    project(head_cols // PROJ_PIECE)
    n_spread = len(pieces)
    chunks_per_head = x_ref.shape[0] // CHUNK
    n_slots = HGRN_HEADS * chunks_per_head
    level = lvl_ref[...]
    odd_row = (lax.broadcasted_iota(jnp.int32, (CHUNK, LANES), 0) & 1) == 1
    for head in range(HGRN_HEADS):
        c0 = head * head_cols
        lb = lb_all[:, head * HGRN_HEAD_DIM:(head + 1) * HGRN_HEAD_DIM]
        st = jnp.where(restart, 0.0, state[head])
        for c in range(chunks_per_head):
            r0 = c * CHUNK
            cols = lambda j: hin[r0:r0 + CHUNK, c0 + j * LANES:c0 + (j + 1) * LANES]
            out, st = _hgrn_chunk(cols(0), cols(1), cols(2), cols(3), st, lb,
                                  gain_ref[...], level, odd_row)
            rec_ref[r0:r0 + CHUNK, head * HGRN_HEAD_DIM:(head + 1) * HGRN_HEAD_DIM] = out
            slot = head * chunks_per_head + c
            due = -(-(slot + 1) * n_spread // n_slots)
            project(due - (n_spread - len(pieces)))
        state[head] = st
    assert not pieces


def _hgrn_chunk(qx, z, vx, gx, st, lb, gain, level, odd_row):
    n_levels = int(math.log2(CHUNK))
    q = qx * jax.nn.sigmoid(qx) * (HGRN_HEAD_DIM ** -0.5)
    v_t = vx.T.astype(BF16)
    t = jnp.exp2(_neg_abs(z) * LOG2_E)
    r = 1.0 / (1.0 + t)
    sig_pos = jnp.where(z >= 0, r, t * r)
    sig_neg = jnp.where(z >= 0, t * r, r)
    f = lb + (1.0 - lb) * sig_pos
    k = (1.0 - lb) * sig_neg
    g = _chunk_cumsum(jnp.log(f) * LOG2_E)
    g_last = g[CHUNK - 1:CHUNK, :]

    q_bf, k_bf = q.astype(BF16), k.astype(BF16)
    a = _dot_nt(q_bf, k_bf)
    a = jnp.where(level == n_levels, a, 0.0)
    for lv in range(n_levels):
        if lv == 0:
            x = jnp.where(odd_row, f, 1.0)
        else:
            x = jnp.exp2(_neg_abs(g - _level_anchor(g, 1 << lv)))
        x = x.astype(BF16)
        a = jnp.where(level == lv, _dot_nt(q_bf * x, k_bf * x), a)

    lhs = jnp.concatenate([a.astype(BF16), (q * jnp.exp2(g)).astype(BF16)], axis=1)
    rhs_t = jnp.concatenate([v_t, st.astype(BF16)], axis=1)
    o = _dot_nt(lhs, rhs_t)
    k_out = (k * jnp.exp2(g_last - g)).astype(BF16)
    st = st * jnp.exp2(g_last) + jnp.dot(v_t, k_out, preferred_element_type=F32)
    return _rms(o, gain) * (gx * jax.nn.sigmoid(gx)), st


def _in_proj_hgrn(x2d, rotary, norm_gain, w_bf16, lb_logits, out_gain, level_map, layer, seq):
    t, d = x2d.shape
    n = w_bf16.shape[1]
    tiles_per_seq = seq // ROW_TILE
    full = lambda a: pl.BlockSpec(a.shape, lambda i: (0,) * a.ndim)
    row = lambda width: pl.BlockSpec((ROW_TILE, width), lambda i: (i, 0))
    pos = pl.BlockSpec((ROW_TILE, LANES), lambda i: (i % tiles_per_seq, 0))
    consts = (norm_gain.reshape(1, d), w_bf16, lb_logits, out_gain.reshape(1, LANES), level_map)
    return pl.pallas_call(
        functools.partial(_in_proj_hgrn_kernel, layer=layer, tiles_per_seq=tiles_per_seq),
        grid=(t // ROW_TILE,),
        in_specs=[row(d)] + [pos] * len(rotary) + [full(a) for a in consts],
        out_specs=[row(3 * ATTN_WIDTH), row(HGRN_WIDTH)],
        out_shape=[jax.ShapeDtypeStruct((t, 3 * ATTN_WIDTH), F32),
                   jax.ShapeDtypeStruct((t, HGRN_WIDTH), F32)],
        scratch_shapes=[pltpu.VMEM((ROW_TILE, n - 3 * ATTN_WIDTH), F32),
                        pltpu.VMEM((HGRN_HEADS, HGRN_HEAD_DIM, HGRN_HEAD_DIM), F32)],
        compiler_params=pltpu.CompilerParams(
            dimension_semantics=("arbitrary",), vmem_limit_bytes=V7X_VMEM_LIMIT),
        name="in_proj_hgrn",
    )(x2d, *rotary, *consts)


def _intra_chunk_levels():
    i = np.arange(CHUNK, dtype=np.int32)[:, None]
    j = np.arange(CHUNK, dtype=np.int32)[None, :]
    x = i ^ j
    lv = np.zeros((CHUNK, CHUNK), np.int32)
    for bit in range(1, int(math.log2(CHUNK))):
        lv = np.where(x >= (1 << bit), bit, lv)
    lv = np.where(i == j, int(math.log2(CHUNK)), lv)
    return jnp.asarray(np.where(j > i, -1, lv).astype(np.int32))


def _mix_mlp_kernel(x_ref, attn_ref, rec_ref, ag_ref, wo_ref, nm_ref, wu_ref, wd_ref,
                    nf_ref, o_ref, *, final_norm):
    attn = _rms(attn_ref[...], ag_ref[...]).astype(BF16)
    x1 = (x_ref[...]
          + jnp.dot(attn, wo_ref[0:ATTN_WIDTH, :], preferred_element_type=F32)
          + jnp.dot(rec_ref[...].astype(BF16), wo_ref[ATTN_WIDTH:MIX_WIDTH, :],
                    preferred_element_type=F32))
    h = _rms(x1, nm_ref[...]).astype(BF16)
    o_ref[...] = x1
    for c0 in range(0, MLP_HIDDEN, COL_CHUNK):
        u = jnp.dot(h, wu_ref[:, c0:c0 + COL_CHUNK], preferred_element_type=F32)
        u = jnp.square(jnp.maximum(u, 0.0)).astype(BF16)
        o_ref[...] += jnp.dot(u, wd_ref[c0:c0 + COL_CHUNK, :], preferred_element_type=F32)
    if final_norm:
        o_ref[...] = _rms(o_ref[...], nf_ref[...])


def _mix_mlp(x2d, attn2d, rec2d, attn_gain, w_out, norm_mlp, w_up, w_down, norm_final,
             final_norm):
    t, d = x2d.shape
    row = lambda w: pl.BlockSpec((ROW_TILE, w), lambda i: (i, 0))
    full = lambda a: pl.BlockSpec(a.shape, lambda i: (0, 0))
    vec = lambda a: a.reshape(1, -1)
    args = (x2d, attn2d, rec2d, vec(attn_gain), w_out, vec(norm_mlp), w_up, w_down,
            vec(norm_final))
    return pl.pallas_call(
        functools.partial(_mix_mlp_kernel, final_norm=final_norm),
        grid=(t // ROW_TILE,),
        in_specs=[row(d), row(ATTN_WIDTH), row(HGRN_WIDTH)] + [full(a) for a in args[3:]],
        out_specs=row(d),
        out_shape=jax.ShapeDtypeStruct((t, d), F32),
        compiler_params=pltpu.CompilerParams(
            dimension_semantics=("arbitrary",), vmem_limit_bytes=V7X_VMEM_LIMIT),
        name="mix_mlp",
    )(*args)


def _rotary_tables(seq):
    half = ATTN_HEAD_DIM // 2
    inv_freq = ROPE_THETA ** (-np.arange(half, dtype=np.float64) / half)
    ang = np.arange(seq, dtype=np.float64)[:, None] * inv_freq[None, :]
    cos, sin = np.cos(ang), np.sin(ang)
    cos = np.concatenate([cos, cos, cos, cos], axis=1)
    sin = np.concatenate([-sin, -sin, sin, sin], axis=1)
    q_scale = ATTN_HEAD_DIM ** -0.5 * LOG2_E
    return tuple(jnp.asarray(t.astype(np.float32))
                 for t in (cos * q_scale, sin * q_scale, cos, sin))


def _arrange_in_proj_columns(w_in_layer):
    d = w_in_layer.shape[0]
    half = ATTN_HEAD_DIM // 2
    qk = w_in_layer[:, :2 * ATTN_WIDTH].reshape(d, 2 * ATTN_WIDTH // LANES, 2, 2, half)
    qk = qk.transpose(0, 1, 3, 2, 4).reshape(d, 2 * ATTN_WIDTH)
    v = w_in_layer[:, 2 * ATTN_WIDTH:3 * ATTN_WIDTH]
    rec = w_in_layer[:, 3 * ATTN_WIDTH:].reshape(d, 4, HGRN_HEADS, HGRN_HEAD_DIM)
    rec = rec.transpose(0, 2, 1, 3).reshape(d, 4 * HGRN_WIDTH)
    return jnp.concatenate([rec, qk, v], axis=1)


def kernel(x, norm_mix, w_in, attn_out_gain, hgrn_lb_logits, hgrn_out_gain, w_out, norm_mlp,
           w_up, w_down, norm_final):
    b, seq, d = x.shape
    depth = w_in.shape[0]
    rotary = _rotary_tables(seq)
    level_map = _intra_chunk_levels()
    x2d = x.reshape(b * seq, d)
    for layer in range(depth):
        qkv, rec = _in_proj_hgrn(x2d, rotary, norm_mix[layer],
                                 _arrange_in_proj_columns(w_in[layer]).astype(BF16),
                                 hgrn_lb_logits, hgrn_out_gain[layer], level_map, layer, seq)
        attn = _dilated_attention(qkv.reshape(b, seq, 3 * ATTN_WIDTH))
        x2d = _mix_mlp(x2d, attn.reshape(b * seq, ATTN_WIDTH), rec,
                       attn_out_gain[layer], w_out[layer].astype(BF16), norm_mlp[layer],
                       w_up[layer].astype(BF16), w_down[layer].astype(BF16), norm_final,
                       final_norm=(layer == depth - 1))
    return x2d.reshape(b, seq, d)
```

```python
import functools
import math

import jax
import jax.numpy as jnp
import numpy as np
from jax import lax
from jax.experimental import pallas as pl
from jax.experimental.pallas import tpu as pltpu

F32 = jnp.float32
BF16 = jnp.bfloat16

D_MODEL = 1024
ATTN_HEADS = 8
ATTN_HEAD_DIM = 64
ATTN_WIDTH = ATTN_HEADS * ATTN_HEAD_DIM
HGRN_HEADS = 4
HGRN_HEAD_DIM = 128
HGRN_WIDTH = HGRN_HEADS * HGRN_HEAD_DIM
MIX_WIDTH = ATTN_WIDTH + HGRN_WIDTH
DILATIONS = (1, 4, 16)
SPAN = 128
ROPE_THETA = 10000.0
MLP_HIDDEN = 4 * D_MODEL
NORM_EPS = 1e-6
MASK_VALUE = -1e30

LANES = 128
V7X_VMEM_LIMIT = 56 * 1024 * 1024
ROW_TILE = 512
COL_CHUNK = 512
CHUNK = 128
ATTN_UNROLL = 4
PROJ_PIECE = 256
LOG2_E = 1.4426950408889634


def _rms(x, gain):
    return x * lax.rsqrt(jnp.mean(x * x, axis=-1, keepdims=True) + NORM_EPS) * gain


def _dot_nt(a, b):
    return lax.dot_general(a, b, (((1,), (1,)), ((), ())), preferred_element_type=F32)


def _attn_kernel(q_ref, k_ref, v_ref, o_ref, out_a, out_b, lse_a, lse_b, p_buf, m_buf,
                 *, seq):
    lane = lax.broadcasted_iota(jnp.int32, (SPAN, LANES), 1)
    head0 = lane < ATTN_HEAD_DIM
    qk_head0 = (lane & (ATTN_HEAD_DIM // 2)) == 0
    q_minus_k = (lax.broadcasted_iota(jnp.int32, (SPAN, SPAN), 0)
                 - lax.broadcasted_iota(jnp.int32, (SPAN, SPAN), 1))
    prev_ok = q_minus_k <= 0
    cur_ok = q_minus_k >= 0
    ones = jnp.ones((SPAN, LANES), BF16)
    n_tiles = seq // SPAN

    assert DILATIONS[0] == 1 and len(DILATIONS) == 3
    parked = {DILATIONS[1]: (out_a, lse_a), DILATIONS[2]: (out_b, lse_b)}

    for dil in sorted(DILATIONS, reverse=True):
        n_blk = n_tiles // dil
        tiles = [(r, n) for r in range(dil) for n in range(n_blk)]

        def block_rows(r, n, dil=dil):
            start = r + dil * SPAN * n
            return pl.ds(start, SPAN) if dil == 1 else pl.ds(start, SPAN, stride=dil)

        def probs(group, block_rows=block_rows):
            key_blocks = {}

            def keys(r, n):
                if (r, n) not in key_blocks:
                    key_blocks[r, n] = k_ref[block_rows(r, n), :].astype(BF16)
                return key_blocks[r, n]

            for slot, (r, n) in enumerate(group):
                qt = q_ref[block_rows(r, n), :]
                kt = keys(r, n) if n == 0 else jnp.concatenate([keys(r, n - 1), keys(r, n)], 0)
                scores = []
                for head in range(2):
                    qh = jnp.where(qk_head0 if head == 0 else ~qk_head0, qt, 0.0).astype(BF16)
                    scores.append(_dot_nt(qh, kt))
                mx = []
                for head, s in enumerate(scores):
                    s_cur = jnp.where(cur_ok, s[:, -SPAN:], MASK_VALUE)
                    if n == 0:
                        m = jnp.max(s_cur, axis=-1, keepdims=True)
                    else:
                        s_prev = jnp.where(prev_ok, s[:, :SPAN], MASK_VALUE)
                        m = jnp.max(jnp.maximum(s_prev, s_cur), axis=-1, keepdims=True)
                        p_buf[slot, head, :, 0:SPAN] = jnp.exp2(s_prev - m).astype(BF16)
                    p_buf[slot, head, :, SPAN:2 * SPAN] = jnp.exp2(s_cur - m).astype(BF16)
                    mx.append(jnp.broadcast_to(m, (SPAN, LANES)))
                m_buf[slot] = jnp.where(head0, mx[0], mx[1])

        def weighted_values(group, block_rows=block_rows, dil=dil):
            value_blocks = {}

            def values(r, n):
                if (r, n) not in value_blocks:
                    value_blocks[r, n] = jnp.concatenate(
                        [v_ref[block_rows(r, n), :].astype(BF16), ones], axis=1)
                return value_blocks[r, n]

            for slot, (r, n) in enumerate(group):
                if n == 0:
                    vt = values(r, n)
                    res = [jnp.dot(p_buf[slot, head, :, SPAN:2 * SPAN], vt,
                                   preferred_element_type=F32) for head in range(2)]
                else:
                    vt = jnp.concatenate([values(r, n - 1), values(r, n)], axis=0)
                    res = [jnp.dot(p_buf[slot, head], vt, preferred_element_type=F32)
                           for head in range(2)]
                denom = jnp.where(head0, res[0][:, LANES:], res[1][:, LANES:])
                out = jnp.where(head0, res[0][:, :LANES], res[1][:, :LANES]) / denom
                lse = m_buf[slot] + jnp.log(denom) * LOG2_E
                q_rows = block_rows(r, n)
                if dil in parked:
                    parked[dil][0][q_rows, :] = out
                    parked[dil][1][q_rows, :] = lse
                else:
                    a, b = lse_a[q_rows, :], lse_b[q_rows, :]
                    top = jnp.maximum(jnp.maximum(lse, a), b)
                    w, wa, wb = jnp.exp2(lse - top), jnp.exp2(a - top), jnp.exp2(b - top)
                    o_ref[q_rows, :] = ((w * out + wa * out_a[q_rows, :] + wb * out_b[q_rows, :])
                                        / (w + wa + wb))

        groups = [tiles[i:i + ATTN_UNROLL] for i in range(0, len(tiles), ATTN_UNROLL)]
        probs(groups[0])
        for done, ahead in zip(groups[:-1], groups[1:]):
            weighted_values(done)
            probs(ahead)
        weighted_values(groups[-1])


def _dilated_attention(qkv4d):
    _, b, seq, _ = qkv4d.shape
    n_pairs = ATTN_WIDTH // LANES
    blk = lambda off: pl.BlockSpec((None, None, seq, LANES), lambda i, j: (off + j, i, 0, 0))
    slab = pltpu.VMEM((seq, LANES), F32)
    return pl.pallas_call(
        functools.partial(_attn_kernel, seq=seq),
        grid=(b, n_pairs),
        in_specs=[blk(0), blk(n_pairs), blk(2 * n_pairs)],
        out_specs=pl.BlockSpec((None, None, seq, LANES), lambda i, j: (j, i, 0, 0)),
        out_shape=jax.ShapeDtypeStruct((n_pairs, b, seq, LANES), F32),
        scratch_shapes=[slab] * 4 + [
            pltpu.VMEM((ATTN_UNROLL, 2, SPAN, 2 * SPAN), BF16),
            pltpu.VMEM((ATTN_UNROLL, SPAN, LANES), F32)],
        compiler_params=pltpu.CompilerParams(
            dimension_semantics=("arbitrary", "arbitrary"),
            vmem_limit_bytes=V7X_VMEM_LIMIT),
        name="dilated_attn",
    )(qkv4d, qkv4d, qkv4d)


def _level_anchor(g, half):
    rows = g.shape[0]
    if 2 * half >= 8:
        g3 = g.reshape(rows // (2 * half), 2 * half, LANES)
        return jnp.broadcast_to(g3[:, half - 1:half, :], g3.shape).reshape(rows, LANES)
    g3 = g.reshape(rows // 8, 8, LANES)
    sub = lax.broadcasted_iota(jnp.int32, g3.shape, 1)
    out = jnp.broadcast_to(g3[:, half - 1:half, :], g3.shape)
    for first in range(2 * half, 8, 2 * half):
        src = jnp.broadcast_to(g3[:, first + half - 1:first + half, :], g3.shape)
        out = jnp.where(sub >= first, src, out)
    return out.reshape(rows, LANES)


def _neg_abs(x):
    return -jnp.abs(x)


def _chunk_cumsum(x):
    row = lax.broadcasted_iota(jnp.int32, x.shape, 0)
    shift = 1
    while shift < x.shape[0]:
        x = x + jnp.where(row >= shift, pltpu.roll(x, shift, axis=0), 0.0)
        shift *= 2
    return x


def _in_proj_hgrn_kernel(x_ref, cq_ref, sq_ref, ck_ref, sk_ref, nm_ref, w_ref, lbl_ref,
                         gain_ref, lvl_ref, qkv_ref, rec_ref, hin, state,
                         *, layer, tiles_per_seq):
    h = _rms(x_ref[...], nm_ref[...]).astype(BF16)
    head_cols = 4 * HGRN_HEAD_DIM
    restart = pl.program_id(0) % tiles_per_seq == 0

    logits = lbl_ref[...]
    e = jnp.exp(logits - jnp.max(logits, axis=0, keepdims=True))
    share = e / jnp.sum(e, axis=0, keepdims=True)
    lb_all = jnp.sum(share[0:layer + 1, :], axis=0, keepdims=True) - share[0:1, :]

    n_rec = HGRN_HEADS * head_cols
    pieces = [(hin, c, c, None) for c in range(0, n_rec, PROJ_PIECE)]
    for c in range(0, 3 * ATTN_WIDTH, PROJ_PIECE):
        tables = ((cq_ref, sq_ref), (ck_ref, sk_ref), None)[c // ATTN_WIDTH]
        pieces.append((qkv_ref, c, n_rec + c, tables))

    def project(n_pieces):
        for _ in range(n_pieces):
            dst, c_dst, c_w, tables = pieces.pop(0)
            y = jnp.dot(h, w_ref[:, c_w:c_w + PROJ_PIECE], preferred_element_type=F32)
            if dst is hin:
                dst[:, c_dst:c_dst + PROJ_PIECE] = y
                continue
            for b0 in range(0, PROJ_PIECE, LANES):
                yb = y[:, b0:b0 + LANES]
                if tables is not None:
                    yb = yb * tables[0][...] + pltpu.roll(yb, LANES // 2, axis=1) * tables[1][...]
                dst[(c_dst + b0) // LANES] = yb

    project(head_cols // PROJ_PIECE)
    n_spread = len(pieces)
    chunks_per_head = x_ref.shape[0] // CHUNK
    n_slots = HGRN_HEADS * chunks_per_head
    level = lvl_ref[...]
    odd_row = (lax.broadcasted_iota(jnp.int32, (CHUNK, LANES), 0) & 1) == 1
    for head in range(HGRN_HEADS):
        c0 = head * head_cols
        lb = lb_all[:, head * HGRN_HEAD_DIM:(head + 1) * HGRN_HEAD_DIM]
        st = jnp.where(restart, 0.0, state[head])
        for c in range(chunks_per_head):
            r0 = c * CHUNK
            cols = lambda j: hin[r0:r0 + CHUNK, c0 + j * LANES:c0 + (j + 1) * LANES]
            out, st = _hgrn_chunk(cols(0), cols(1), cols(2), cols(3), st, lb,
                                  gain_ref[...], level, odd_row)
            rec_ref[r0:r0 + CHUNK, head * HGRN_HEAD_DIM:(head + 1) * HGRN_HEAD_DIM] = out
            slot = head * chunks_per_head + c
            due = -(-(slot + 1) * n_spread // n_slots)
            project(due - (n_spread - len(pieces)))
        state[head] = st
    assert not pieces


def _hgrn_chunk(qx, z, vx, gx, st, lb, gain, level, odd_row):
    n_levels = int(math.log2(CHUNK))
    q = qx * jax.nn.sigmoid(qx) * (HGRN_HEAD_DIM ** -0.5)
    v_t = vx.T.astype(BF16)
    t = jnp.exp2(_neg_abs(z) * LOG2_E)
    r = 1.0 / (1.0 + t)
    sig_pos = jnp.where(z >= 0, r, t * r)
    sig_neg = jnp.where(z >= 0, t * r, r)
    f = lb + (1.0 - lb) * sig_pos
    k = (1.0 - lb) * sig_neg
    g = _chunk_cumsum(jnp.log(f) * LOG2_E)
    g_last = g[CHUNK - 1:CHUNK, :]

    q_bf, k_bf = q.astype(BF16), k.astype(BF16)
    a = _dot_nt(q_bf, k_bf)
    a = jnp.where(level == n_levels, a, 0.0)
    for lv in range(n_levels):
        if lv == 0:
            x = jnp.where(odd_row, f, 1.0)
        else:
            x = jnp.exp2(_neg_abs(g - _level_anchor(g, 1 << lv)))
        x = x.astype(BF16)
        a = jnp.where(level == lv, _dot_nt(q_bf * x, k_bf * x), a)

    lhs = jnp.concatenate([a.astype(BF16), (q * jnp.exp2(g)).astype(BF16)], axis=1)
    rhs_t = jnp.concatenate([v_t, st.astype(BF16)], axis=1)
    o = _dot_nt(lhs, rhs_t)
    k_out = (k * jnp.exp2(g_last - g)).astype(BF16)
    st = st * jnp.exp2(g_last) + jnp.dot(v_t, k_out, preferred_element_type=F32)
    return _rms(o, gain) * (gx * jax.nn.sigmoid(gx)), st


def _in_proj_hgrn(x2d, rotary, norm_gain, w_bf16, lb_logits, out_gain, level_map, layer, seq):
    t, d = x2d.shape
    n = w_bf16.shape[1]
    tiles_per_seq = seq // ROW_TILE
    n_blocks = 3 * ATTN_WIDTH // LANES
    full = lambda a: pl.BlockSpec(a.shape, lambda i: (0,) * a.ndim)
    row = lambda width: pl.BlockSpec((ROW_TILE, width), lambda i: (i, 0))
    pos = pl.BlockSpec((ROW_TILE, LANES), lambda i: (i % tiles_per_seq, 0))
    consts = (norm_gain.reshape(1, d), w_bf16, lb_logits, out_gain.reshape(1, LANES), level_map)
    return pl.pallas_call(
        functools.partial(_in_proj_hgrn_kernel, layer=layer, tiles_per_seq=tiles_per_seq),
        grid=(t // ROW_TILE,),
        in_specs=[row(d)] + [pos] * len(rotary) + [full(a) for a in consts],
        out_specs=[pl.BlockSpec((n_blocks, ROW_TILE, LANES), lambda i: (0, i, 0)),
                   row(HGRN_WIDTH)],
        out_shape=[jax.ShapeDtypeStruct((n_blocks, t, LANES), F32),
                   jax.ShapeDtypeStruct((t, HGRN_WIDTH), F32)],
        scratch_shapes=[pltpu.VMEM((ROW_TILE, n - 3 * ATTN_WIDTH), F32),
                        pltpu.VMEM((HGRN_HEADS, HGRN_HEAD_DIM, HGRN_HEAD_DIM), F32)],
        compiler_params=pltpu.CompilerParams(
            dimension_semantics=("arbitrary",), vmem_limit_bytes=V7X_VMEM_LIMIT),
        name="in_proj_hgrn",
    )(x2d, *rotary, *consts)


def _intra_chunk_levels():
    i = np.arange(CHUNK, dtype=np.int32)[:, None]
    j = np.arange(CHUNK, dtype=np.int32)[None, :]
    x = i ^ j
    lv = np.zeros((CHUNK, CHUNK), np.int32)
    for bit in range(1, int(math.log2(CHUNK))):
        lv = np.where(x >= (1 << bit), bit, lv)
    lv = np.where(i == j, int(math.log2(CHUNK)), lv)
    return jnp.asarray(np.where(j > i, -1, lv).astype(np.int32))


def _mix_mlp_kernel(x_ref, attn_ref, rec_ref, ag_ref, wo_ref, nm_ref, wu_ref, wd_ref,
                    nf_ref, o_ref, *, final_norm):
    attn = jnp.concatenate([attn_ref[c] for c in range(attn_ref.shape[0])], axis=1)
    attn = _rms(attn, ag_ref[...]).astype(BF16)
    x1 = (x_ref[...]
          + jnp.dot(attn, wo_ref[0:ATTN_WIDTH, :], preferred_element_type=F32)
          + jnp.dot(rec_ref[...].astype(BF16), wo_ref[ATTN_WIDTH:MIX_WIDTH, :],
                    preferred_element_type=F32))
    h = _rms(x1, nm_ref[...]).astype(BF16)
    o_ref[...] = x1
    for c0 in range(0, MLP_HIDDEN, COL_CHUNK):
        u = jnp.dot(h, wu_ref[:, c0:c0 + COL_CHUNK], preferred_element_type=F32)
        u = jnp.square(jnp.maximum(u, 0.0)).astype(BF16)
        o_ref[...] += jnp.dot(u, wd_ref[c0:c0 + COL_CHUNK, :], preferred_element_type=F32)
    if final_norm:
        o_ref[...] = _rms(o_ref[...], nf_ref[...])


def _mix_mlp(x2d, attn3d, rec2d, attn_gain, w_out, norm_mlp, w_up, w_down, norm_final,
             final_norm):
    t, d = x2d.shape
    row = lambda w: pl.BlockSpec((ROW_TILE, w), lambda i: (i, 0))
    full = lambda a: pl.BlockSpec(a.shape, lambda i: (0, 0))
    vec = lambda a: a.reshape(1, -1)
    args = (x2d, attn3d, rec2d, vec(attn_gain), w_out, vec(norm_mlp), w_up, w_down,
            vec(norm_final))
    return pl.pallas_call(
        functools.partial(_mix_mlp_kernel, final_norm=final_norm),
        grid=(t // ROW_TILE,),
        in_specs=[row(d), pl.BlockSpec((attn3d.shape[0], ROW_TILE, LANES), lambda i: (0, i, 0)),
                  row(HGRN_WIDTH)] + [full(a) for a in args[3:]],
        out_specs=row(d),
        out_shape=jax.ShapeDtypeStruct((t, d), F32),
        compiler_params=pltpu.CompilerParams(
            dimension_semantics=("arbitrary",), vmem_limit_bytes=V7X_VMEM_LIMIT),
        name="mix_mlp",
    )(*args)


def _rotary_tables(seq):
    half = ATTN_HEAD_DIM // 2
    inv_freq = ROPE_THETA ** (-np.arange(half, dtype=np.float64) / half)
    ang = np.arange(seq, dtype=np.float64)[:, None] * inv_freq[None, :]
    cos, sin = np.cos(ang), np.sin(ang)
    cos = np.concatenate([cos, cos, cos, cos], axis=1)
    sin = np.concatenate([-sin, -sin, sin, sin], axis=1)
    q_scale = ATTN_HEAD_DIM ** -0.5 * LOG2_E
    return tuple(jnp.asarray(t.astype(np.float32))
                 for t in (cos * q_scale, sin * q_scale, cos, sin))


def _arrange_in_proj_columns(w_in_layer):
    d = w_in_layer.shape[0]
    half = ATTN_HEAD_DIM // 2
    qk = w_in_layer[:, :2 * ATTN_WIDTH].reshape(d, 2 * ATTN_WIDTH // LANES, 2, 2, half)
    qk = qk.transpose(0, 1, 3, 2, 4).reshape(d, 2 * ATTN_WIDTH)
    v = w_in_layer[:, 2 * ATTN_WIDTH:3 * ATTN_WIDTH]
    rec = w_in_layer[:, 3 * ATTN_WIDTH:].reshape(d, 4, HGRN_HEADS, HGRN_HEAD_DIM)
    rec = rec.transpose(0, 2, 1, 3).reshape(d, 4 * HGRN_WIDTH)
    return jnp.concatenate([rec, qk, v], axis=1)


def kernel(x, norm_mix, w_in, attn_out_gain, hgrn_lb_logits, hgrn_out_gain, w_out, norm_mlp,
           w_up, w_down, norm_final):
    b, seq, d = x.shape
    depth = w_in.shape[0]
    rotary = _rotary_tables(seq)
    level_map = _intra_chunk_levels()
    x2d = x.reshape(b * seq, d)
    for layer in range(depth):
        qkv, rec = _in_proj_hgrn(x2d, rotary, norm_mix[layer],
                                 _arrange_in_proj_columns(w_in[layer]).astype(BF16),
                                 hgrn_lb_logits, hgrn_out_gain[layer], level_map, layer, seq)
        attn = _dilated_attention(qkv.reshape(-1, b, seq, LANES))
        x2d = _mix_mlp(x2d, attn.reshape(-1, b * seq, LANES), rec,
                       attn_out_gain[layer], w_out[layer].astype(BF16), norm_mlp[layer],
                       w_up[layer].astype(BF16), w_down[layer].astype(BF16), norm_final,
                       final_norm=(layer == depth - 1))
    return x2d.reshape(b, seq, d)
```

```python
import functools
import math

import jax
import jax.numpy as jnp
import numpy as np
from jax import lax
from jax.experimental import pallas as pl
from jax.experimental.pallas import tpu as pltpu

F32 = jnp.float32
BF16 = jnp.bfloat16

D_MODEL = 1024
ATTN_HEADS = 8
ATTN_HEAD_DIM = 64
ATTN_WIDTH = ATTN_HEADS * ATTN_HEAD_DIM
HGRN_HEADS = 4
HGRN_HEAD_DIM = 128
HGRN_WIDTH = HGRN_HEADS * HGRN_HEAD_DIM
MIX_WIDTH = ATTN_WIDTH + HGRN_WIDTH
DILATIONS = (1, 4, 16)
SPAN = 128
ROPE_THETA = 10000.0
MLP_HIDDEN = 4 * D_MODEL
NORM_EPS = 1e-6
MASK_VALUE = -1e30

LANES = 128
V7X_VMEM_LIMIT = 56 * 1024 * 1024
ROW_TILE = 512
COL_CHUNK = 512
CHUNK = 128
ATTN_UNROLL = 4
PROJ_PIECE = 256
LOG2_E = 1.4426950408889634


def _rms(x, gain):
    return x * lax.rsqrt(jnp.mean(x * x, axis=-1, keepdims=True) + NORM_EPS) * gain


def _dot_nt(a, b):
    return lax.dot_general(a, b, (((1,), (1,)), ((), ())), preferred_element_type=F32)


def _attn_kernel(q_hbm_blk, k_hbm_blk, v_hbm_blk, o_ref, q_ref, k_ref, v_ref,
                 out_a, out_b, lse_a, lse_b, p_buf, m_buf, *, seq):
    def widen(t, carry):
        rows = pl.ds(pl.multiple_of(t * SPAN, SPAN), SPAN)
        q_ref[rows, :] = q_hbm_blk[rows, :].astype(F32)
        k_ref[rows, :] = k_hbm_blk[rows, :].astype(F32)
        v_ref[rows, :] = v_hbm_blk[rows, :].astype(F32)
        return carry

    lax.fori_loop(0, seq // SPAN, widen, 0, unroll=4)

    lane = lax.broadcasted_iota(jnp.int32, (SPAN, LANES), 1)
    head0 = lane < ATTN_HEAD_DIM
    qk_head0 = (lane & (ATTN_HEAD_DIM // 2)) == 0
    q_minus_k = (lax.broadcasted_iota(jnp.int32, (SPAN, SPAN), 0)
                 - lax.broadcasted_iota(jnp.int32, (SPAN, SPAN), 1))
    prev_ok = q_minus_k <= 0
    cur_ok = q_minus_k >= 0
    ones = jnp.ones((SPAN, LANES), BF16)
    n_tiles = seq // SPAN

    assert DILATIONS[0] == 1 and len(DILATIONS) == 3
    parked = {DILATIONS[1]: (out_a, lse_a), DILATIONS[2]: (out_b, lse_b)}

    for dil in sorted(DILATIONS, reverse=True):
        n_blk = n_tiles // dil
        tiles = [(r, n) for r in range(dil) for n in range(n_blk)]

        def block_rows(r, n, dil=dil):
            start = r + dil * SPAN * n
            return pl.ds(start, SPAN) if dil == 1 else pl.ds(start, SPAN, stride=dil)

        def probs(group, block_rows=block_rows):
            key_blocks = {}

            def keys(r, n):
                if (r, n) not in key_blocks:
                    key_blocks[r, n] = k_ref[block_rows(r, n), :].astype(BF16)
                return key_blocks[r, n]

            for slot, (r, n) in enumerate(group):
                qt = q_ref[block_rows(r, n), :]
                kt = keys(r, n) if n == 0 else jnp.concatenate([keys(r, n - 1), keys(r, n)], 0)
                scores = []
                for head in range(2):
                    qh = jnp.where(qk_head0 if head == 0 else ~qk_head0, qt, 0.0).astype(BF16)
                    scores.append(_dot_nt(qh, kt))
                mx = []
                for head, s in enumerate(scores):
                    s_cur = jnp.where(cur_ok, s[:, -SPAN:], MASK_VALUE)
                    if n == 0:
                        m = jnp.max(s_cur, axis=-1, keepdims=True)
                    else:
                        s_prev = jnp.where(prev_ok, s[:, :SPAN], MASK_VALUE)
                        m = jnp.max(jnp.maximum(s_prev, s_cur), axis=-1, keepdims=True)
                        p_buf[slot, head, :, 0:SPAN] = jnp.exp2(s_prev - m).astype(BF16)
                    p_buf[slot, head, :, SPAN:2 * SPAN] = jnp.exp2(s_cur - m).astype(BF16)
                    mx.append(jnp.broadcast_to(m, (SPAN, LANES)))
                m_buf[slot] = jnp.where(head0, mx[0], mx[1])

        def weighted_values(group, block_rows=block_rows, dil=dil):
            value_blocks = {}

            def values(r, n):
                if (r, n) not in value_blocks:
                    value_blocks[r, n] = jnp.concatenate(
                        [v_ref[block_rows(r, n), :].astype(BF16), ones], axis=1)
                return value_blocks[r, n]

            for slot, (r, n) in enumerate(group):
                if n == 0:
                    vt = values(r, n)
                    res = [jnp.dot(p_buf[slot, head, :, SPAN:2 * SPAN], vt,
                                   preferred_element_type=F32) for head in range(2)]
                else:
                    vt = jnp.concatenate([values(r, n - 1), values(r, n)], axis=0)
                    res = [jnp.dot(p_buf[slot, head], vt, preferred_element_type=F32)
                           for head in range(2)]
                denom = jnp.where(head0, res[0][:, LANES:], res[1][:, LANES:])
                out = jnp.where(head0, res[0][:, :LANES], res[1][:, :LANES]) / denom
                lse = m_buf[slot] + jnp.log(denom) * LOG2_E
                q_rows = block_rows(r, n)
                if dil in parked:
                    parked[dil][0][q_rows, :] = out
                    parked[dil][1][q_rows, :] = lse
                else:
                    a, b = lse_a[q_rows, :], lse_b[q_rows, :]
                    top = jnp.maximum(jnp.maximum(lse, a), b)
                    w, wa, wb = jnp.exp2(lse - top), jnp.exp2(a - top), jnp.exp2(b - top)
                    merged = ((w * out + wa * out_a[q_rows, :] + wb * out_b[q_rows, :])
                              / (w + wa + wb))
                    o_ref[q_rows, :] = merged.astype(o_ref.dtype)

        groups = [tiles[i:i + ATTN_UNROLL] for i in range(0, len(tiles), ATTN_UNROLL)]
        probs(groups[0])
        for done, ahead in zip(groups[:-1], groups[1:]):
            weighted_values(done)
            probs(ahead)
        weighted_values(groups[-1])


def _dilated_attention(qkv3d):
    b, seq, _ = qkv3d.shape
    n_pairs = ATTN_WIDTH // LANES
    blk = lambda off: pl.BlockSpec((None, seq, LANES), lambda i, j: (i, 0, off + j))
    slab = pltpu.VMEM((seq, LANES), F32)
    return pl.pallas_call(
        functools.partial(_attn_kernel, seq=seq),
        grid=(b, n_pairs),
        in_specs=[blk(0), blk(n_pairs), blk(2 * n_pairs)],
        out_specs=pl.BlockSpec((None, seq, LANES), lambda i, j: (i, 0, j)),
        out_shape=jax.ShapeDtypeStruct((b, seq, ATTN_WIDTH), BF16),
        scratch_shapes=[slab] * 7 + [
            pltpu.VMEM((ATTN_UNROLL, 2, SPAN, 2 * SPAN), BF16),
            pltpu.VMEM((ATTN_UNROLL, SPAN, LANES), F32)],
        compiler_params=pltpu.CompilerParams(
            dimension_semantics=("arbitrary", "arbitrary"),
            vmem_limit_bytes=V7X_VMEM_LIMIT),
        name="dilated_attn",
    )(qkv3d, qkv3d, qkv3d)


def _level_anchor(g, half):
    rows = g.shape[0]
    if 2 * half >= 8:
        g3 = g.reshape(rows // (2 * half), 2 * half, LANES)
        return jnp.broadcast_to(g3[:, half - 1:half, :], g3.shape).reshape(rows, LANES)
    g3 = g.reshape(rows // 8, 8, LANES)
    sub = lax.broadcasted_iota(jnp.int32, g3.shape, 1)
    out = jnp.broadcast_to(g3[:, half - 1:half, :], g3.shape)
    for first in range(2 * half, 8, 2 * half):
        src = jnp.broadcast_to(g3[:, first + half - 1:first + half, :], g3.shape)
        out = jnp.where(sub >= first, src, out)
    return out.reshape(rows, LANES)


def _neg_abs(x):
    return -jnp.abs(x)


def _chunk_cumsum(x):
    row = lax.broadcasted_iota(jnp.int32, x.shape, 0)
    shift = 1
    while shift < x.shape[0]:
        x = x + jnp.where(row >= shift, pltpu.roll(x, shift, axis=0), 0.0)
        shift *= 2
    return x


def _in_proj_hgrn_kernel(x_ref, cq_ref, sq_ref, ck_ref, sk_ref, nm_ref, w_ref, lbl_ref,
                         gain_ref, lvl_ref, qkv_ref, rec_ref, hin, state,
                         *, layer, tiles_per_seq):
    h = _rms(x_ref[...], nm_ref[...]).astype(BF16)
    head_cols = 4 * HGRN_HEAD_DIM
    restart = pl.program_id(0) % tiles_per_seq == 0

    logits = lbl_ref[...]
    e = jnp.exp(logits - jnp.max(logits, axis=0, keepdims=True))
    share = e / jnp.sum(e, axis=0, keepdims=True)
    lb_all = jnp.sum(share[0:layer + 1, :], axis=0, keepdims=True) - share[0:1, :]

    n_rec = HGRN_HEADS * head_cols
    pieces = [(hin, c, c, None) for c in range(0, n_rec, PROJ_PIECE)]
    for c in range(0, 3 * ATTN_WIDTH, PROJ_PIECE):
        tables = ((cq_ref, sq_ref), (ck_ref, sk_ref), None)[c // ATTN_WIDTH]
        pieces.append((qkv_ref, c, n_rec + c, tables))

    def project(n_pieces):
        for _ in range(n_pieces):
            dst, c_dst, c_w, tables = pieces.pop(0)
            y = jnp.dot(h, w_ref[:, c_w:c_w + PROJ_PIECE], preferred_element_type=F32)
            if tables is None:
                dst[:, c_dst:c_dst + PROJ_PIECE] = y.astype(dst.dtype)
                continue
            cos, sin = tables[0][...], tables[1][...]
            for b0 in range(0, PROJ_PIECE, LANES):
                yb = y[:, b0:b0 + LANES]
                dst[:, c_dst + b0:c_dst + b0 + LANES] = (
                    yb * cos + pltpu.roll(yb, LANES // 2, axis=1) * sin).astype(dst.dtype)

    project(head_cols // PROJ_PIECE)
    n_spread = len(pieces)
    chunks_per_head = x_ref.shape[0] // CHUNK
    n_slots = HGRN_HEADS * chunks_per_head
    level = lvl_ref[...]
    odd_row = (lax.broadcasted_iota(jnp.int32, (CHUNK, LANES), 0) & 1) == 1
    for head in range(HGRN_HEADS):
        c0 = head * head_cols
        lb = lb_all[:, head * HGRN_HEAD_DIM:(head + 1) * HGRN_HEAD_DIM]
        st = jnp.where(restart, 0.0, state[head])
        for c in range(chunks_per_head):
            r0 = c * CHUNK
            cols = lambda j: hin[r0:r0 + CHUNK, c0 + j * LANES:c0 + (j + 1) * LANES]
            out, st = _hgrn_chunk(cols(0), cols(1), cols(2), cols(3), st, lb,
                                  gain_ref[...], level, odd_row)
            rec_ref[r0:r0 + CHUNK, head * HGRN_HEAD_DIM:(head + 1) * HGRN_HEAD_DIM] = (
                out.astype(rec_ref.dtype))
            slot = head * chunks_per_head + c
            due = -(-(slot + 1) * n_spread // n_slots)
            project(due - (n_spread - len(pieces)))
        state[head] = st
    assert not pieces


def _hgrn_chunk(qx, z, vx, gx, st, lb, gain, level, odd_row):
    n_levels = int(math.log2(CHUNK))
    q = qx * jax.nn.sigmoid(qx) * (HGRN_HEAD_DIM ** -0.5)
    v_t = vx.T.astype(BF16)
    t = jnp.exp2(_neg_abs(z) * LOG2_E)
    r = 1.0 / (1.0 + t)
    sig_pos = jnp.where(z >= 0, r, t * r)
    sig_neg = jnp.where(z >= 0, t * r, r)
    f = lb + (1.0 - lb) * sig_pos
    k = (1.0 - lb) * sig_neg
    g = _chunk_cumsum(jnp.log(f) * LOG2_E)
    g_last = g[CHUNK - 1:CHUNK, :]

    q_bf, k_bf = q.astype(BF16), k.astype(BF16)
    a = _dot_nt(q_bf, k_bf)
    a = jnp.where(level == n_levels, a, 0.0)
    for lv in range(n_levels):
        if lv == 0:
            x = jnp.where(odd_row, f, 1.0)
        else:
            x = jnp.exp2(_neg_abs(g - _level_anchor(g, 1 << lv)))
        x = x.astype(BF16)
        a = jnp.where(level == lv, _dot_nt(q_bf * x, k_bf * x), a)

    lhs = jnp.concatenate([a.astype(BF16), (q * jnp.exp2(g)).astype(BF16)], axis=1)
    rhs_t = jnp.concatenate([v_t, st.astype(BF16)], axis=1)
    o = _dot_nt(lhs, rhs_t)
    k_out = (k * jnp.exp2(g_last - g)).astype(BF16)
    st = st * jnp.exp2(g_last) + jnp.dot(v_t, k_out, preferred_element_type=F32)
    return _rms(o, gain) * (gx * jax.nn.sigmoid(gx)), st


def _in_proj_hgrn(x2d, rotary, norm_gain, w_bf16, lb_logits, out_gain, level_map, layer, seq):
    t, d = x2d.shape
    n = w_bf16.shape[1]
    tiles_per_seq = seq // ROW_TILE
    full = lambda a: pl.BlockSpec(a.shape, lambda i: (0,) * a.ndim)
    row = lambda width: pl.BlockSpec((ROW_TILE, width), lambda i: (i, 0))
    pos = pl.BlockSpec((ROW_TILE, LANES), lambda i: (i % tiles_per_seq, 0))
    consts = (norm_gain.reshape(1, d), w_bf16, lb_logits, out_gain.reshape(1, LANES), level_map)
    return pl.pallas_call(
        functools.partial(_in_proj_hgrn_kernel, layer=layer, tiles_per_seq=tiles_per_seq),
        grid=(t // ROW_TILE,),
        in_specs=[row(d)] + [pos] * len(rotary) + [full(a) for a in consts],
        out_specs=[row(3 * ATTN_WIDTH), row(HGRN_WIDTH)],
        out_shape=[jax.ShapeDtypeStruct((t, 3 * ATTN_WIDTH), BF16),
                   jax.ShapeDtypeStruct((t, HGRN_WIDTH), BF16)],
        scratch_shapes=[pltpu.VMEM((ROW_TILE, n - 3 * ATTN_WIDTH), F32),
                        pltpu.VMEM((HGRN_HEADS, HGRN_HEAD_DIM, HGRN_HEAD_DIM), F32)],
        compiler_params=pltpu.CompilerParams(
            dimension_semantics=("arbitrary",), vmem_limit_bytes=V7X_VMEM_LIMIT),
        name="in_proj_hgrn",
    )(x2d, *rotary, *consts)


def _intra_chunk_levels():
    i = np.arange(CHUNK, dtype=np.int32)[:, None]
    j = np.arange(CHUNK, dtype=np.int32)[None, :]
    x = i ^ j
    lv = np.zeros((CHUNK, CHUNK), np.int32)
    for bit in range(1, int(math.log2(CHUNK))):
        lv = np.where(x >= (1 << bit), bit, lv)
    lv = np.where(i == j, int(math.log2(CHUNK)), lv)
    return jnp.asarray(np.where(j > i, -1, lv).astype(np.int32))


def _mix_mlp_kernel(x_ref, attn_ref, rec_ref, ag_ref, wo_ref, nm_ref, wu_ref, wd_ref,
                    nf_ref, o_ref, *, final_norm):
    attn = _rms(attn_ref[...].astype(F32), ag_ref[...]).astype(BF16)
    x1 = (x_ref[...]
          + jnp.dot(attn, wo_ref[0:ATTN_WIDTH, :], preferred_element_type=F32)
          + jnp.dot(rec_ref[...], wo_ref[ATTN_WIDTH:MIX_WIDTH, :],
                    preferred_element_type=F32))
    h = _rms(x1, nm_ref[...]).astype(BF16)
    o_ref[...] = x1
    for c0 in range(0, MLP_HIDDEN, COL_CHUNK):
        u = jnp.dot(h, wu_ref[:, c0:c0 + COL_CHUNK], preferred_element_type=F32)
        u = jnp.square(jnp.maximum(u, 0.0)).astype(BF16)
        o_ref[...] += jnp.dot(u, wd_ref[c0:c0 + COL_CHUNK, :], preferred_element_type=F32)
    if final_norm:
        o_ref[...] = _rms(o_ref[...], nf_ref[...])


def _mix_mlp(x2d, attn2d, rec2d, attn_gain, w_out, norm_mlp, w_up, w_down, norm_final,
             final_norm):
    t, d = x2d.shape
    row = lambda w: pl.BlockSpec((ROW_TILE, w), lambda i: (i, 0))
    full = lambda a: pl.BlockSpec(a.shape, lambda i: (0, 0))
    vec = lambda a: a.reshape(1, -1)
    args = (x2d, attn2d, rec2d, vec(attn_gain), w_out, vec(norm_mlp), w_up, w_down,
            vec(norm_final))
    return pl.pallas_call(
        functools.partial(_mix_mlp_kernel, final_norm=final_norm),
        grid=(t // ROW_TILE,),
        in_specs=[row(d), row(ATTN_WIDTH), row(HGRN_WIDTH)] + [full(a) for a in args[3:]],
        out_specs=row(d),
        out_shape=jax.ShapeDtypeStruct((t, d), F32),
        compiler_params=pltpu.CompilerParams(
            dimension_semantics=("arbitrary",), vmem_limit_bytes=V7X_VMEM_LIMIT),
        name="mix_mlp",
    )(*args)


def _rotary_tables(seq):
    half = ATTN_HEAD_DIM // 2
    inv_freq = ROPE_THETA ** (-np.arange(half, dtype=np.float64) / half)
    ang = np.arange(seq, dtype=np.float64)[:, None] * inv_freq[None, :]
    cos, sin = np.cos(ang), np.sin(ang)
    cos = np.concatenate([cos, cos, cos, cos], axis=1)
    sin = np.concatenate([-sin, -sin, sin, sin], axis=1)
    q_scale = ATTN_HEAD_DIM ** -0.5 * LOG2_E
    return tuple(jnp.asarray(t.astype(np.float32))
                 for t in (cos * q_scale, sin * q_scale, cos, sin))


def _arrange_in_proj_columns(w_in_layer):
    d = w_in_layer.shape[0]
    half = ATTN_HEAD_DIM // 2
    qk = w_in_layer[:, :2 * ATTN_WIDTH].reshape(d, 2 * ATTN_WIDTH // LANES, 2, 2, half)
    qk = qk.transpose(0, 1, 3, 2, 4).reshape(d, 2 * ATTN_WIDTH)
    v = w_in_layer[:, 2 * ATTN_WIDTH:3 * ATTN_WIDTH]
    rec = w_in_layer[:, 3 * ATTN_WIDTH:].reshape(d, 4, HGRN_HEADS, HGRN_HEAD_DIM)
    rec = rec.transpose(0, 2, 1, 3).reshape(d, 4 * HGRN_WIDTH)
    return jnp.concatenate([rec, qk, v], axis=1)


def kernel(x, norm_mix, w_in, attn_out_gain, hgrn_lb_logits, hgrn_out_gain, w_out, norm_mlp,
           w_up, w_down, norm_final):
    b, seq, d = x.shape
    depth = w_in.shape[0]
    rotary = _rotary_tables(seq)
    level_map = _intra_chunk_levels()
    x2d = x.reshape(b * seq, d)
    for layer in range(depth):
        qkv, rec = _in_proj_hgrn(x2d, rotary, norm_mix[layer],
                                 _arrange_in_proj_columns(w_in[layer]).astype(BF16),
                                 hgrn_lb_logits, hgrn_out_gain[layer], level_map, layer, seq)
        attn = _dilated_attention(qkv.reshape(b, seq, 3 * ATTN_WIDTH))
        x2d = _mix_mlp(x2d, attn.reshape(b * seq, ATTN_WIDTH), rec,
                       attn_out_gain[layer], w_out[layer].astype(BF16), norm_mlp[layer],
                       w_up[layer].astype(BF16), w_down[layer].astype(BF16), norm_final,
                       final_norm=(layer == depth - 1))
    return x2d.reshape(b, seq, d)
```

```python
import functools
import math

import jax
import jax.numpy as jnp
import numpy as np
from jax import lax
from jax.experimental import pallas as pl
from jax.experimental.pallas import tpu as pltpu

F32 = jnp.float32
BF16 = jnp.bfloat16

D_MODEL = 1024
ATTN_HEADS = 8
ATTN_HEAD_DIM = 64
ATTN_WIDTH = ATTN_HEADS * ATTN_HEAD_DIM
HGRN_HEADS = 4
HGRN_HEAD_DIM = 128
HGRN_WIDTH = HGRN_HEADS * HGRN_HEAD_DIM
MIX_WIDTH = ATTN_WIDTH + HGRN_WIDTH
DILATIONS = (1, 4, 16)
SPAN = 128
ROPE_THETA = 10000.0
MLP_HIDDEN = 4 * D_MODEL
NORM_EPS = 1e-6
MASK_VALUE = -1e30

LANES = 128
V7X_VMEM_LIMIT = 56 * 1024 * 1024
ROW_TILE = 512
COL_CHUNK = 512
CHUNK = 128
ATTN_UNROLL = 4
PROJ_PIECE = 256
LOG2_E = 1.4426950408889634


def _rms(x, gain):
    return x * lax.rsqrt(jnp.mean(x * x, axis=-1, keepdims=True) + NORM_EPS) * gain


def _dot_nt(a, b):
    return lax.dot_general(a, b, (((1,), (1,)), ((), ())), preferred_element_type=F32)


def _attn_kernel(q_ref, k_ref, v_ref, o_ref, out_a, out_b, lse_a, lse_b, p_buf, m_buf,
                 *, seq):
    lane = lax.broadcasted_iota(jnp.int32, (SPAN, LANES), 1)
    head0 = lane < ATTN_HEAD_DIM
    q_minus_k = (lax.broadcasted_iota(jnp.int32, (SPAN, SPAN), 0)
                 - lax.broadcasted_iota(jnp.int32, (SPAN, SPAN), 1))
    prev_ok = q_minus_k <= 0
    cur_ok = q_minus_k >= 0
    ones = jnp.ones((SPAN, LANES), BF16)
    n_tiles = seq // SPAN

    assert DILATIONS[0] == 1 and len(DILATIONS) == 3
    parked = {DILATIONS[1]: (out_a, lse_a), DILATIONS[2]: (out_b, lse_b)}

    for dil in sorted(DILATIONS, reverse=True):
        n_blk = n_tiles // dil
        tiles = [(r, n) for r in range(dil) for n in range(n_blk)]

        def block_rows(r, n, dil=dil):
            start = r + dil * SPAN * n
            return pl.ds(start, SPAN) if dil == 1 else pl.ds(start, SPAN, stride=dil)

        def probs(group, block_rows=block_rows):
            key_blocks = {}

            def keys(r, n):
                if (r, n) not in key_blocks:
                    key_blocks[r, n] = k_ref[block_rows(r, n), :].astype(BF16)
                return key_blocks[r, n]

            for slot, (r, n) in enumerate(group):
                qt = q_ref[block_rows(r, n), :]
                kt = keys(r, n) if n == 0 else jnp.concatenate([keys(r, n - 1), keys(r, n)], 0)
                scores = []
                for head in range(2):
                    qh = jnp.where(head0 if head == 0 else ~head0, qt, 0.0).astype(BF16)
                    scores.append(_dot_nt(qh, kt))
                mx = []
                for head, s in enumerate(scores):
                    s_cur = jnp.where(cur_ok, s[:, -SPAN:], MASK_VALUE)
                    if n == 0:
                        m = jnp.max(s_cur, axis=-1, keepdims=True)
                    else:
                        s_prev = jnp.where(prev_ok, s[:, :SPAN], MASK_VALUE)
                        m = jnp.max(jnp.maximum(s_prev, s_cur), axis=-1, keepdims=True)
                        p_buf[slot, head, :, 0:SPAN] = jnp.exp2(s_prev - m).astype(BF16)
                    p_buf[slot, head, :, SPAN:2 * SPAN] = jnp.exp2(s_cur - m).astype(BF16)
                    mx.append(jnp.broadcast_to(m, (SPAN, LANES)))
                m_buf[slot] = jnp.where(head0, mx[0], mx[1])

        def weighted_values(group, block_rows=block_rows, dil=dil):
            value_blocks = {}

            def values(r, n):
                if (r, n) not in value_blocks:
                    value_blocks[r, n] = jnp.concatenate(
                        [v_ref[block_rows(r, n), :].astype(BF16), ones], axis=1)
                return value_blocks[r, n]

            for slot, (r, n) in enumerate(group):
                if n == 0:
                    vt = values(r, n)
                    res = [jnp.dot(p_buf[slot, head, :, SPAN:2 * SPAN], vt,
                                   preferred_element_type=F32) for head in range(2)]
                else:
                    vt = jnp.concatenate([values(r, n - 1), values(r, n)], axis=0)
                    res = [jnp.dot(p_buf[slot, head], vt, preferred_element_type=F32)
                           for head in range(2)]
                denom = jnp.where(head0, res[0][:, LANES:], res[1][:, LANES:])
                out = jnp.where(head0, res[0][:, :LANES], res[1][:, :LANES]) / denom
                lse = m_buf[slot] + jnp.log(denom) * LOG2_E
                q_rows = block_rows(r, n)
                if dil in parked:
                    parked[dil][0][q_rows, :] = out
                    parked[dil][1][q_rows, :] = lse
                else:
                    a, b = lse_a[q_rows, :], lse_b[q_rows, :]
                    top = jnp.maximum(jnp.maximum(lse, a), b)
                    w, wa, wb = jnp.exp2(lse - top), jnp.exp2(a - top), jnp.exp2(b - top)
                    o_ref[q_rows, :] = ((w * out + wa * out_a[q_rows, :] + wb * out_b[q_rows, :])
                                        / (w + wa + wb))

        groups = [tiles[i:i + ATTN_UNROLL] for i in range(0, len(tiles), ATTN_UNROLL)]
        probs(groups[0])
        for done, ahead in zip(groups[:-1], groups[1:]):
            weighted_values(done)
            probs(ahead)
        weighted_values(groups[-1])


def _dilated_attention(qkv3d):
    b, seq, _ = qkv3d.shape
    n_pairs = ATTN_WIDTH // LANES
    blk = lambda off: pl.BlockSpec((None, seq, LANES), lambda i, j: (i, 0, off + j))
    slab = pltpu.VMEM((seq, LANES), F32)
    return pl.pallas_call(
        functools.partial(_attn_kernel, seq=seq),
        grid=(b, n_pairs),
        in_specs=[blk(0), blk(n_pairs), blk(2 * n_pairs)],
        out_specs=pl.BlockSpec((None, seq, LANES), lambda i, j: (i, 0, j)),
        out_shape=jax.ShapeDtypeStruct((b, seq, ATTN_WIDTH), F32),
        scratch_shapes=[slab] * 4 + [
            pltpu.VMEM((ATTN_UNROLL, 2, SPAN, 2 * SPAN), BF16),
            pltpu.VMEM((ATTN_UNROLL, SPAN, LANES), F32)],
        compiler_params=pltpu.CompilerParams(
            dimension_semantics=("arbitrary", "arbitrary"),
            vmem_limit_bytes=V7X_VMEM_LIMIT),
        name="dilated_attn",
    )(qkv3d, qkv3d, qkv3d)


def _level_anchor(g, half):
    rows = g.shape[0]
    if 2 * half >= 8:
        g3 = g.reshape(rows // (2 * half), 2 * half, LANES)
        return jnp.broadcast_to(g3[:, half - 1:half, :], g3.shape).reshape(rows, LANES)
    g3 = g.reshape(rows // 8, 8, LANES)
    sub = lax.broadcasted_iota(jnp.int32, g3.shape, 1)
    out = jnp.broadcast_to(g3[:, half - 1:half, :], g3.shape)
    for first in range(2 * half, 8, 2 * half):
        src = jnp.broadcast_to(g3[:, first + half - 1:first + half, :], g3.shape)
        out = jnp.where(sub >= first, src, out)
    return out.reshape(rows, LANES)


def _neg_abs(x):
    return -jnp.abs(x)


def _chunk_cumsum(x):
    row = lax.broadcasted_iota(jnp.int32, x.shape, 0)
    shift = 1
    while shift < x.shape[0]:
        x = x + jnp.where(row >= shift, pltpu.roll(x, shift, axis=0), 0.0)
        shift *= 2
    return x


def _in_proj_hgrn_kernel(x_ref, cq_ref, sq_ref, ck_ref, sk_ref, nm_ref, w_ref, lbl_ref,
                         gain_ref, lvl_ref, qkv_ref, rec_ref, hin, state,
                         *, layer, tiles_per_seq):
    h = _rms(x_ref[...], nm_ref[...]).astype(BF16)
    head_cols = 4 * HGRN_HEAD_DIM
    restart = pl.program_id(0) % tiles_per_seq == 0

    logits = lbl_ref[...]
    e = jnp.exp(logits - jnp.max(logits, axis=0, keepdims=True))
    share = e / jnp.sum(e, axis=0, keepdims=True)
    lb_all = jnp.sum(share[0:layer + 1, :], axis=0, keepdims=True) - share[0:1, :]

    def rec_cols(head, kind):
        c = 3 * ATTN_WIDTH + kind * HGRN_WIDTH + head * HGRN_HEAD_DIM
        return slice(c, c + HGRN_HEAD_DIM)

    blocks = PROJ_PIECE // LANES
    pieces = []
    for head in range(HGRN_HEADS):
        for kind in range(0, 4, blocks):
            pieces.append((hin, head * head_cols + kind * LANES,
                           [rec_cols(head, kind + i) for i in range(blocks)], None))
    for c in range(0, 3 * ATTN_WIDTH, PROJ_PIECE):
        tables = ((cq_ref, sq_ref), (ck_ref, sk_ref), None)[c // ATTN_WIDTH]
        pieces.append((qkv_ref, c, [slice(c, c + PROJ_PIECE)], tables))

    lane = lax.broadcasted_iota(jnp.int32, (x_ref.shape[0], LANES), 1)
    first_half = (lane & (ATTN_HEAD_DIM // 2)) == 0

    def project(n_pieces):
        for _ in range(n_pieces):
            dst, c_dst, w_cols, tables = pieces.pop(0)
            w = jnp.concatenate([w_ref[:, cols] for cols in w_cols], axis=1)
            y = jnp.dot(h, w, preferred_element_type=F32)
            if tables is None:
                dst[:, c_dst:c_dst + PROJ_PIECE] = y
                continue
            cos, sin = tables[0][...], tables[1][...]
            for b0 in range(0, PROJ_PIECE, LANES):
                yb = y[:, b0:b0 + LANES]
                partner = jnp.where(first_half,
                                    pltpu.roll(yb, LANES - ATTN_HEAD_DIM // 2, axis=1),
                                    pltpu.roll(yb, ATTN_HEAD_DIM // 2, axis=1))
                dst[:, c_dst + b0:c_dst + b0 + LANES] = yb * cos + partner * sin

    project(head_cols // PROJ_PIECE)
    n_spread = len(pieces)
    chunks_per_head = x_ref.shape[0] // CHUNK
    n_slots = HGRN_HEADS * chunks_per_head
    level = lvl_ref[...]
    odd_row = (lax.broadcasted_iota(jnp.int32, (CHUNK, LANES), 0) & 1) == 1
    for head in range(HGRN_HEADS):
        c0 = head * head_cols
        lb = lb_all[:, head * HGRN_HEAD_DIM:(head + 1) * HGRN_HEAD_DIM]
        st = jnp.where(restart, 0.0, state[head])
        for c in range(chunks_per_head):
            r0 = c * CHUNK
            cols = lambda j: hin[r0:r0 + CHUNK, c0 + j * LANES:c0 + (j + 1) * LANES]
            out, st = _hgrn_chunk(cols(0), cols(1), cols(2), cols(3), st, lb,
                                  gain_ref[...], level, odd_row)
            rec_ref[r0:r0 + CHUNK, head * HGRN_HEAD_DIM:(head + 1) * HGRN_HEAD_DIM] = out
            slot = head * chunks_per_head + c
            due = -(-(slot + 1) * n_spread // n_slots)
            project(due - (n_spread - len(pieces)))
        state[head] = st
    assert not pieces


def _hgrn_chunk(qx, z, vx, gx, st, lb, gain, level, odd_row):
    n_levels = int(math.log2(CHUNK))
    q = qx * jax.nn.sigmoid(qx) * (HGRN_HEAD_DIM ** -0.5)
    v_t = vx.T.astype(BF16)
    t = jnp.exp2(_neg_abs(z) * LOG2_E)
    r = 1.0 / (1.0 + t)
    sig_pos = jnp.where(z >= 0, r, t * r)
    sig_neg = jnp.where(z >= 0, t * r, r)
    f = lb + (1.0 - lb) * sig_pos
    k = (1.0 - lb) * sig_neg
    g = _chunk_cumsum(jnp.log(f) * LOG2_E)
    g_last = g[CHUNK - 1:CHUNK, :]

    q_bf, k_bf = q.astype(BF16), k.astype(BF16)
    a = _dot_nt(q_bf, k_bf)
    a = jnp.where(level == n_levels, a, 0.0)
    for lv in range(n_levels):
        if lv == 0:
            x = jnp.where(odd_row, f, 1.0)
        else:
            x = jnp.exp2(_neg_abs(g - _level_anchor(g, 1 << lv)))
        x = x.astype(BF16)
        a = jnp.where(level == lv, _dot_nt(q_bf * x, k_bf * x), a)

    lhs = jnp.concatenate([a.astype(BF16), (q * jnp.exp2(g)).astype(BF16)], axis=1)
    rhs_t = jnp.concatenate([v_t, st.astype(BF16)], axis=1)
    o = _dot_nt(lhs, rhs_t)
    k_out = (k * jnp.exp2(g_last - g)).astype(BF16)
    st = st * jnp.exp2(g_last) + jnp.dot(v_t, k_out, preferred_element_type=F32)
    return _rms(o, gain) * (gx * jax.nn.sigmoid(gx)), st


def _in_proj_hgrn(x2d, rotary, norm_gain, w_bf16, lb_logits, out_gain, level_map, layer, seq):
    t, d = x2d.shape
    n = w_bf16.shape[1]
    tiles_per_seq = seq // ROW_TILE
    full = lambda a: pl.BlockSpec(a.shape, lambda i: (0,) * a.ndim)
    row = lambda width: pl.BlockSpec((ROW_TILE, width), lambda i: (i, 0))
    pos = pl.BlockSpec((ROW_TILE, LANES), lambda i: (i % tiles_per_seq, 0))
    consts = (norm_gain.reshape(1, d), w_bf16, lb_logits, out_gain.reshape(1, LANES), level_map)
    return pl.pallas_call(
        functools.partial(_in_proj_hgrn_kernel, layer=layer, tiles_per_seq=tiles_per_seq),
        grid=(t // ROW_TILE,),
        in_specs=[row(d)] + [pos] * len(rotary) + [full(a) for a in consts],
        out_specs=[row(3 * ATTN_WIDTH), row(HGRN_WIDTH)],
        out_shape=[jax.ShapeDtypeStruct((t, 3 * ATTN_WIDTH), F32),
                   jax.ShapeDtypeStruct((t, HGRN_WIDTH), F32)],
        scratch_shapes=[pltpu.VMEM((ROW_TILE, n - 3 * ATTN_WIDTH), F32),
                        pltpu.VMEM((HGRN_HEADS, HGRN_HEAD_DIM, HGRN_HEAD_DIM), F32)],
        compiler_params=pltpu.CompilerParams(
            dimension_semantics=("arbitrary",), vmem_limit_bytes=V7X_VMEM_LIMIT),
        name="in_proj_hgrn",
    )(x2d, *rotary, *consts)


def _intra_chunk_levels():
    i = np.arange(CHUNK, dtype=np.int32)[:, None]
    j = np.arange(CHUNK, dtype=np.int32)[None, :]
    x = i ^ j
    lv = np.zeros((CHUNK, CHUNK), np.int32)
    for bit in range(1, int(math.log2(CHUNK))):
        lv = np.where(x >= (1 << bit), bit, lv)
    lv = np.where(i == j, int(math.log2(CHUNK)), lv)
    return jnp.asarray(np.where(j > i, -1, lv).astype(np.int32))


def _mix_mlp_kernel(x_ref, attn_ref, rec_ref, ag_ref, wo_ref, nm_ref, wu_ref, wd_ref,
                    nf_ref, o_ref, *, final_norm):
    attn = _rms(attn_ref[...], ag_ref[...]).astype(BF16)
    x1 = (x_ref[...]
          + jnp.dot(attn, wo_ref[0:ATTN_WIDTH, :], preferred_element_type=F32)
          + jnp.dot(rec_ref[...].astype(BF16), wo_ref[ATTN_WIDTH:MIX_WIDTH, :],
                    preferred_element_type=F32))
    h = _rms(x1, nm_ref[...]).astype(BF16)
    o_ref[...] = x1
    for c0 in range(0, MLP_HIDDEN, COL_CHUNK):
        u = jnp.dot(h, wu_ref[:, c0:c0 + COL_CHUNK], preferred_element_type=F32)
        u = jnp.square(jnp.maximum(u, 0.0)).astype(BF16)
        o_ref[...] += jnp.dot(u, wd_ref[c0:c0 + COL_CHUNK, :], preferred_element_type=F32)
    if final_norm:
        o_ref[...] = _rms(o_ref[...], nf_ref[...])


def _mix_mlp(x2d, attn2d, rec2d, attn_gain, w_out, norm_mlp, w_up, w_down, norm_final,
             final_norm):
    t, d = x2d.shape
    row = lambda w: pl.BlockSpec((ROW_TILE, w), lambda i: (i, 0))
    full = lambda a: pl.BlockSpec(a.shape, lambda i: (0, 0))
    vec = lambda a: a.reshape(1, -1)
    args = (x2d, attn2d, rec2d, vec(attn_gain), w_out, vec(norm_mlp), w_up, w_down,
            vec(norm_final))
    return pl.pallas_call(
        functools.partial(_mix_mlp_kernel, final_norm=final_norm),
        grid=(t // ROW_TILE,),
        in_specs=[row(d), row(ATTN_WIDTH), row(HGRN_WIDTH)] + [full(a) for a in args[3:]],
        out_specs=row(d),
        out_shape=jax.ShapeDtypeStruct((t, d), F32),
        compiler_params=pltpu.CompilerParams(
            dimension_semantics=("arbitrary",), vmem_limit_bytes=V7X_VMEM_LIMIT),
        name="mix_mlp",
    )(*args)


def _rotary_tables(seq):
    half = ATTN_HEAD_DIM // 2
    inv_freq = ROPE_THETA ** (-np.arange(half, dtype=np.float64) / half)
    ang = np.arange(seq, dtype=np.float64)[:, None] * inv_freq[None, :]
    cos, sin = np.cos(ang), np.sin(ang)
    cos = np.concatenate([cos, cos, cos, cos], axis=1)
    sin = np.concatenate([-sin, sin, -sin, sin], axis=1)
    q_scale = ATTN_HEAD_DIM ** -0.5 * LOG2_E
    return tuple(jnp.asarray(t.astype(np.float32))
                 for t in (cos * q_scale, sin * q_scale, cos, sin))


def kernel(x, norm_mix, w_in, attn_out_gain, hgrn_lb_logits, hgrn_out_gain, w_out, norm_mlp,
           w_up, w_down, norm_final):
    b, seq, d = x.shape
    depth = w_in.shape[0]
    rotary = _rotary_tables(seq)
    level_map = _intra_chunk_levels()
    x2d = x.reshape(b * seq, d)
    w_in_bf16 = w_in.astype(BF16)
    for layer in range(depth):
        qkv, rec = _in_proj_hgrn(x2d, rotary, norm_mix[layer], w_in_bf16[layer],
                                 hgrn_lb_logits, hgrn_out_gain[layer], level_map, layer, seq)
        attn = _dilated_attention(qkv.reshape(b, seq, 3 * ATTN_WIDTH))
        x2d = _mix_mlp(x2d, attn.reshape(b * seq, ATTN_WIDTH), rec,
                       attn_out_gain[layer], w_out[layer].astype(BF16), norm_mlp[layer],
                       w_up[layer].astype(BF16), w_down[layer].astype(BF16), norm_final,
                       final_norm=(layer == depth - 1))
    return x2d.reshape(b, seq, d)
```

```python
import functools
import math

import jax
import jax.numpy as jnp
import numpy as np
from jax import lax
from jax.experimental import pallas as pl
from jax.experimental.pallas import tpu as pltpu

F32 = jnp.float32
BF16 = jnp.bfloat16

D_MODEL = 1024
ATTN_HEADS = 8
ATTN_HEAD_DIM = 64
ATTN_WIDTH = ATTN_HEADS * ATTN_HEAD_DIM
HGRN_HEADS = 4
HGRN_HEAD_DIM = 128
HGRN_WIDTH = HGRN_HEADS * HGRN_HEAD_DIM
MIX_WIDTH = ATTN_WIDTH + HGRN_WIDTH
DILATIONS = (1, 4, 16)
SPAN = 128
ROPE_THETA = 10000.0
MLP_HIDDEN = 4 * D_MODEL
NORM_EPS = 1e-6
MASK_VALUE = -1e30

LANES = 128
V7X_VMEM_LIMIT = 56 * 1024 * 1024
ROW_TILE = 512
COL_CHUNK = 512
CHUNK = 128
ATTN_UNROLL = 4
PROJ_PIECE = 256
LOG2_E = 1.4426950408889634


def _whole_array_spec(a, layer):
    if a.ndim == 2:
        return pl.BlockSpec(a.shape, lambda i: (0, 0))
    return pl.BlockSpec((None,) + a.shape[1:], lambda i: (layer, 0, 0))


def _rms(x, gain):
    return x * lax.rsqrt(jnp.mean(x * x, axis=-1, keepdims=True) + NORM_EPS) * gain


def _dot_nt(a, b):
    return lax.dot_general(a, b, (((1,), (1,)), ((), ())), preferred_element_type=F32)


def _attn_kernel(q_ref, k_ref, v_ref, o_ref, out_a, out_b, lse_a, lse_b, p_buf, m_buf,
                 *, seq):
    lane = lax.broadcasted_iota(jnp.int32, (SPAN, LANES), 1)
    head0 = lane < ATTN_HEAD_DIM
    q_minus_k = (lax.broadcasted_iota(jnp.int32, (SPAN, SPAN), 0)
                 - lax.broadcasted_iota(jnp.int32, (SPAN, SPAN), 1))
    prev_ok = q_minus_k <= 0
    cur_ok = q_minus_k >= 0
    ones = jnp.ones((SPAN, LANES), BF16)
    n_tiles = seq // SPAN

    assert DILATIONS[0] == 1 and len(DILATIONS) == 3
    parked = {DILATIONS[1]: (out_a, lse_a), DILATIONS[2]: (out_b, lse_b)}

    for dil in sorted(DILATIONS, reverse=True):
        n_blk = n_tiles // dil
        tiles = [(r, n) for r in range(dil) for n in range(n_blk)]

        def block_rows(r, n, dil=dil):
            start = r + dil * SPAN * n
            return pl.ds(start, SPAN) if dil == 1 else pl.ds(start, SPAN, stride=dil)

        def probs(group, block_rows=block_rows):
            key_blocks = {}

            def keys(r, n):
                if (r, n) not in key_blocks:
                    key_blocks[r, n] = k_ref[block_rows(r, n), :].astype(BF16)
                return key_blocks[r, n]

            for slot, (r, n) in enumerate(group):
                qt = q_ref[block_rows(r, n), :]
                kt = keys(r, n) if n == 0 else jnp.concatenate([keys(r, n - 1), keys(r, n)], 0)
                scores = []
                for head in range(2):
                    qh = jnp.where(head0 if head == 0 else ~head0, qt, 0.0).astype(BF16)
                    scores.append(_dot_nt(qh, kt))
                mx = []
                for head, s in enumerate(scores):
                    s_cur = jnp.where(cur_ok, s[:, -SPAN:], MASK_VALUE)
                    if n == 0:
                        m = jnp.max(s_cur, axis=-1, keepdims=True)
                    else:
                        s_prev = jnp.where(prev_ok, s[:, :SPAN], MASK_VALUE)
                        m = jnp.max(jnp.maximum(s_prev, s_cur), axis=-1, keepdims=True)
                        p_buf[slot, head, :, 0:SPAN] = jnp.exp2(s_prev - m).astype(BF16)
                    p_buf[slot, head, :, SPAN:2 * SPAN] = jnp.exp2(s_cur - m).astype(BF16)
                    mx.append(jnp.broadcast_to(m, (SPAN, LANES)))
                m_buf[slot] = jnp.where(head0, mx[0], mx[1])

        def weighted_values(group, block_rows=block_rows, dil=dil):
            value_blocks = {}

            def values(r, n):
                if (r, n) not in value_blocks:
                    value_blocks[r, n] = jnp.concatenate(
                        [v_ref[block_rows(r, n), :].astype(BF16), ones], axis=1)
                return value_blocks[r, n]

            for slot, (r, n) in enumerate(group):
                if n == 0:
                    vt = values(r, n)
                    res = [jnp.dot(p_buf[slot, head, :, SPAN:2 * SPAN], vt,
                                   preferred_element_type=F32) for head in range(2)]
                else:
                    vt = jnp.concatenate([values(r, n - 1), values(r, n)], axis=0)
                    res = [jnp.dot(p_buf[slot, head], vt, preferred_element_type=F32)
                           for head in range(2)]
                denom = jnp.where(head0, res[0][:, LANES:], res[1][:, LANES:])
                out = jnp.where(head0, res[0][:, :LANES], res[1][:, :LANES]) / denom
                lse = m_buf[slot] + jnp.log(denom) * LOG2_E
                q_rows = block_rows(r, n)
                if dil in parked:
                    parked[dil][0][q_rows, :] = out
                    parked[dil][1][q_rows, :] = lse
                else:
                    a, b = lse_a[q_rows, :], lse_b[q_rows, :]
                    top = jnp.maximum(jnp.maximum(lse, a), b)
                    w, wa, wb = jnp.exp2(lse - top), jnp.exp2(a - top), jnp.exp2(b - top)
                    o_ref[q_rows, :] = ((w * out + wa * out_a[q_rows, :] + wb * out_b[q_rows, :])
                                        / (w + wa + wb))

        groups = [tiles[i:i + ATTN_UNROLL] for i in range(0, len(tiles), ATTN_UNROLL)]
        probs(groups[0])
        for done, ahead in zip(groups[:-1], groups[1:]):
            weighted_values(done)
            probs(ahead)
        weighted_values(groups[-1])


def _dilated_attention(qkv3d):
    b, seq, _ = qkv3d.shape
    n_pairs = ATTN_WIDTH // LANES
    blk = lambda off: pl.BlockSpec((None, seq, LANES), lambda i, j: (i, 0, off + j))
    slab = pltpu.VMEM((seq, LANES), F32)
    return pl.pallas_call(
        functools.partial(_attn_kernel, seq=seq),
        grid=(b, n_pairs),
        in_specs=[blk(0), blk(n_pairs), blk(2 * n_pairs)],
        out_specs=pl.BlockSpec((None, seq, LANES), lambda i, j: (i, 0, j)),
        out_shape=jax.ShapeDtypeStruct((b, seq, ATTN_WIDTH), F32),
        scratch_shapes=[slab] * 4 + [
            pltpu.VMEM((ATTN_UNROLL, 2, SPAN, 2 * SPAN), BF16),
            pltpu.VMEM((ATTN_UNROLL, SPAN, LANES), F32)],
        compiler_params=pltpu.CompilerParams(
            dimension_semantics=("arbitrary", "arbitrary"),
            vmem_limit_bytes=V7X_VMEM_LIMIT),
        name="dilated_attn",
    )(qkv3d, qkv3d, qkv3d)


def _level_anchor(g, half):
    rows = g.shape[0]
    if 2 * half >= 8:
        g3 = g.reshape(rows // (2 * half), 2 * half, LANES)
        return jnp.broadcast_to(g3[:, half - 1:half, :], g3.shape).reshape(rows, LANES)
    g3 = g.reshape(rows // 8, 8, LANES)
    sub = lax.broadcasted_iota(jnp.int32, g3.shape, 1)
    out = jnp.broadcast_to(g3[:, half - 1:half, :], g3.shape)
    for first in range(2 * half, 8, 2 * half):
        src = jnp.broadcast_to(g3[:, first + half - 1:first + half, :], g3.shape)
        out = jnp.where(sub >= first, src, out)
    return out.reshape(rows, LANES)


def _neg_abs(x):
    return -jnp.abs(x)


def _chunk_cumsum(x):
    row = lax.broadcasted_iota(jnp.int32, x.shape, 0)
    shift = 1
    while shift < x.shape[0]:
        x = x + jnp.where(row >= shift, pltpu.roll(x, shift, axis=0), 0.0)
        shift *= 2
    return x


def _in_proj_hgrn_kernel(x_ref, cq_ref, sq_ref, ck_ref, sk_ref, nm_ref, w_ref, lbl_ref,
                         gain_ref, lvl_ref, qkv_ref, rec_ref, hin, state,
                         *, layer, tiles_per_seq):
    h = _rms(x_ref[...], nm_ref[...]).astype(BF16)
    head_cols = 4 * HGRN_HEAD_DIM
    restart = pl.program_id(0) % tiles_per_seq == 0

    logits = lbl_ref[...]
    e = jnp.exp(logits - jnp.max(logits, axis=0, keepdims=True))
    share = e / jnp.sum(e, axis=0, keepdims=True)
    lb_all = jnp.sum(share[0:layer + 1, :], axis=0, keepdims=True) - share[0:1, :]

    def rec_cols(head, kind):
        c = 3 * ATTN_WIDTH + kind * HGRN_WIDTH + head * HGRN_HEAD_DIM
        return slice(c, c + HGRN_HEAD_DIM)

    blocks = PROJ_PIECE // LANES
    pieces = []
    for head in range(HGRN_HEADS):
        for kind in range(0, 4, blocks):
            pieces.append((hin, head * head_cols + kind * LANES,
                           [rec_cols(head, kind + i) for i in range(blocks)], None))
    for c in range(0, 3 * ATTN_WIDTH, PROJ_PIECE):
        tables = ((cq_ref, sq_ref), (ck_ref, sk_ref), None)[c // ATTN_WIDTH]
        pieces.append((qkv_ref, c, [slice(c, c + PROJ_PIECE)], tables))

    lane = lax.broadcasted_iota(jnp.int32, (x_ref.shape[0], LANES), 1)
    first_half = (lane & (ATTN_HEAD_DIM // 2)) == 0

    def project(n_pieces):
        for _ in range(n_pieces):
            dst, c_dst, w_cols, tables = pieces.pop(0)
            w = jnp.concatenate([w_ref[:, cols] for cols in w_cols], axis=1)
            y = jnp.dot(h, w, preferred_element_type=F32)
            if tables is None:
                dst[:, c_dst:c_dst + PROJ_PIECE] = y
                continue
            cos, sin = tables[0][...], tables[1][...]
            for b0 in range(0, PROJ_PIECE, LANES):
                yb = y[:, b0:b0 + LANES]
                partner = jnp.where(first_half,
                                    pltpu.roll(yb, LANES - ATTN_HEAD_DIM // 2, axis=1),
                                    pltpu.roll(yb, ATTN_HEAD_DIM // 2, axis=1))
                dst[:, c_dst + b0:c_dst + b0 + LANES] = yb * cos + partner * sin

    project(head_cols // PROJ_PIECE)
    n_spread = len(pieces)
    chunks_per_head = x_ref.shape[0] // CHUNK
    n_slots = HGRN_HEADS * chunks_per_head
    level = lvl_ref[...]
    odd_row = (lax.broadcasted_iota(jnp.int32, (CHUNK, LANES), 0) & 1) == 1
    for head in range(HGRN_HEADS):
        c0 = head * head_cols
        lb = lb_all[:, head * HGRN_HEAD_DIM:(head + 1) * HGRN_HEAD_DIM]
        st = jnp.where(restart, 0.0, state[head])
        for c in range(chunks_per_head):
            r0 = c * CHUNK
            cols = lambda j: hin[r0:r0 + CHUNK, c0 + j * LANES:c0 + (j + 1) * LANES]
            out, st = _hgrn_chunk(cols(0), cols(1), cols(2), cols(3), st, lb,
                                  gain_ref[...], level, odd_row)
            rec_ref[r0:r0 + CHUNK, head * HGRN_HEAD_DIM:(head + 1) * HGRN_HEAD_DIM] = out
            slot = head * chunks_per_head + c
            due = -(-(slot + 1) * n_spread // n_slots)
            project(due - (n_spread - len(pieces)))
        state[head] = st
    assert not pieces


def _hgrn_chunk(qx, z, vx, gx, st, lb, gain, level, odd_row):
    n_levels = int(math.log2(CHUNK))
    q = qx * jax.nn.sigmoid(qx) * (HGRN_HEAD_DIM ** -0.5)
    v_t = vx.T.astype(BF16)
    t = jnp.exp2(_neg_abs(z) * LOG2_E)
    r = 1.0 / (1.0 + t)
    sig_pos = jnp.where(z >= 0, r, t * r)
    sig_neg = jnp.where(z >= 0, t * r, r)
    f = lb + (1.0 - lb) * sig_pos
    k = (1.0 - lb) * sig_neg
    g = _chunk_cumsum(jnp.log(f) * LOG2_E)
    g_last = g[CHUNK - 1:CHUNK, :]

    q_bf, k_bf = q.astype(BF16), k.astype(BF16)
    a = _dot_nt(q_bf, k_bf)
    a = jnp.where(level == n_levels, a, 0.0)
    for lv in range(n_levels):
        if lv == 0:
            x = jnp.where(odd_row, f, 1.0)
        else:
            x = jnp.exp2(_neg_abs(g - _level_anchor(g, 1 << lv)))
        x = x.astype(BF16)
        a = jnp.where(level == lv, _dot_nt(q_bf * x, k_bf * x), a)

    lhs = jnp.concatenate([a.astype(BF16), (q * jnp.exp2(g)).astype(BF16)], axis=1)
    rhs_t = jnp.concatenate([v_t, st.astype(BF16)], axis=1)
    o = _dot_nt(lhs, rhs_t)
    k_out = (k * jnp.exp2(g_last - g)).astype(BF16)
    st = st * jnp.exp2(g_last) + jnp.dot(v_t, k_out, preferred_element_type=F32)
    return _rms(o, gain) * (gx * jax.nn.sigmoid(gx)), st


def _in_proj_hgrn(x2d, rotary, norm_gain, w_bf16, lb_logits, out_gain, level_map, layer, seq):
    t, d = x2d.shape
    n = w_bf16.shape[-1]
    tiles_per_seq = seq // ROW_TILE
    full = functools.partial(_whole_array_spec, layer=layer)
    row = lambda width: pl.BlockSpec((ROW_TILE, width), lambda i: (i, 0))
    pos = pl.BlockSpec((ROW_TILE, LANES), lambda i: (i % tiles_per_seq, 0))
    consts = (norm_gain.reshape(1, d), w_bf16, lb_logits, out_gain.reshape(1, LANES), level_map)
    return pl.pallas_call(
        functools.partial(_in_proj_hgrn_kernel, layer=layer, tiles_per_seq=tiles_per_seq),
        grid=(t // ROW_TILE,),
        in_specs=[row(d)] + [pos] * len(rotary) + [full(a) for a in consts],
        out_specs=[row(3 * ATTN_WIDTH), row(HGRN_WIDTH)],
        out_shape=[jax.ShapeDtypeStruct((t, 3 * ATTN_WIDTH), F32),
                   jax.ShapeDtypeStruct((t, HGRN_WIDTH), F32)],
        scratch_shapes=[pltpu.VMEM((ROW_TILE, n - 3 * ATTN_WIDTH), F32),
                        pltpu.VMEM((HGRN_HEADS, HGRN_HEAD_DIM, HGRN_HEAD_DIM), F32)],
        compiler_params=pltpu.CompilerParams(
            dimension_semantics=("arbitrary",), vmem_limit_bytes=V7X_VMEM_LIMIT),
        name="in_proj_hgrn",
    )(x2d, *rotary, *consts)


def _intra_chunk_levels():
    i = np.arange(CHUNK, dtype=np.int32)[:, None]
    j = np.arange(CHUNK, dtype=np.int32)[None, :]
    x = i ^ j
    lv = np.zeros((CHUNK, CHUNK), np.int32)
    for bit in range(1, int(math.log2(CHUNK))):
        lv = np.where(x >= (1 << bit), bit, lv)
    lv = np.where(i == j, int(math.log2(CHUNK)), lv)
    return jnp.asarray(np.where(j > i, -1, lv).astype(np.int32))


def _mix_mlp_kernel(x_ref, attn_ref, rec_ref, ag_ref, wo_ref, nm_ref, wu_ref, wd_ref,
                    nf_ref, o_ref, *, final_norm):
    attn = _rms(attn_ref[...], ag_ref[...]).astype(BF16)
    x1 = (x_ref[...]
          + jnp.dot(attn, wo_ref[0:ATTN_WIDTH, :], preferred_element_type=F32)
          + jnp.dot(rec_ref[...].astype(BF16), wo_ref[ATTN_WIDTH:MIX_WIDTH, :],
                    preferred_element_type=F32))
    h = _rms(x1, nm_ref[...]).astype(BF16)
    o_ref[...] = x1
    for c0 in range(0, MLP_HIDDEN, COL_CHUNK):
        u = jnp.dot(h, wu_ref[:, c0:c0 + COL_CHUNK], preferred_element_type=F32)
        u = jnp.square(jnp.maximum(u, 0.0)).astype(BF16)
        o_ref[...] += jnp.dot(u, wd_ref[c0:c0 + COL_CHUNK, :], preferred_element_type=F32)
    if final_norm:
        o_ref[...] = _rms(o_ref[...], nf_ref[...])


def _mix_mlp(x2d, attn2d, rec2d, attn_gain, w_out, norm_mlp, w_up, w_down, norm_final,
             layer, final_norm):
    t, d = x2d.shape
    row = lambda w: pl.BlockSpec((ROW_TILE, w), lambda i: (i, 0))
    full = functools.partial(_whole_array_spec, layer=layer)
    vec = lambda a: a.reshape(1, -1)
    args = (x2d, attn2d, rec2d, vec(attn_gain), w_out, vec(norm_mlp), w_up, w_down,
            vec(norm_final))
    return pl.pallas_call(
        functools.partial(_mix_mlp_kernel, final_norm=final_norm),
        grid=(t // ROW_TILE,),
        in_specs=[row(d), row(ATTN_WIDTH), row(HGRN_WIDTH)] + [full(a) for a in args[3:]],
        out_specs=row(d),
        out_shape=jax.ShapeDtypeStruct((t, d), F32),
        compiler_params=pltpu.CompilerParams(
            dimension_semantics=("arbitrary",), vmem_limit_bytes=V7X_VMEM_LIMIT),
        name="mix_mlp",
    )(*args)


def _rotary_tables(seq):
    half = ATTN_HEAD_DIM // 2
    inv_freq = ROPE_THETA ** (-np.arange(half, dtype=np.float64) / half)
    ang = np.arange(seq, dtype=np.float64)[:, None] * inv_freq[None, :]
    cos, sin = np.cos(ang), np.sin(ang)
    cos = np.concatenate([cos, cos, cos, cos], axis=1)
    sin = np.concatenate([-sin, sin, -sin, sin], axis=1)
    q_scale = ATTN_HEAD_DIM ** -0.5 * LOG2_E
    return tuple(jnp.asarray(t.astype(np.float32))
                 for t in (cos * q_scale, sin * q_scale, cos, sin))


def kernel(x, norm_mix, w_in, attn_out_gain, hgrn_lb_logits, hgrn_out_gain, w_out, norm_mlp,
           w_up, w_down, norm_final):
    b, seq, d = x.shape
    depth = w_in.shape[0]
    rotary = _rotary_tables(seq)
    level_map = _intra_chunk_levels()
    x2d = x.reshape(b * seq, d)
    w_in, w_out, w_up, w_down = (w.astype(BF16) for w in (w_in, w_out, w_up, w_down))
    for layer in range(depth):
        qkv, rec = _in_proj_hgrn(x2d, rotary, norm_mix[layer], w_in,
                                 hgrn_lb_logits, hgrn_out_gain[layer], level_map, layer, seq)
        attn = _dilated_attention(qkv.reshape(b, seq, 3 * ATTN_WIDTH))
        x2d = _mix_mlp(x2d, attn.reshape(b * seq, ATTN_WIDTH), rec,
                       attn_out_gain[layer], w_out, norm_mlp[layer], w_up, w_down, norm_final,
                       layer, final_norm=(layer == depth - 1))
    return x2d.reshape(b, seq, d)
```

```python
import functools
import math

import jax
import jax.numpy as jnp
import numpy as np
from jax import lax
from jax.experimental import pallas as pl
from jax.experimental.pallas import tpu as pltpu

F32 = jnp.float32
BF16 = jnp.bfloat16

D_MODEL = 1024
ATTN_HEADS = 8
ATTN_HEAD_DIM = 64
ATTN_WIDTH = ATTN_HEADS * ATTN_HEAD_DIM
HGRN_HEADS = 4
HGRN_HEAD_DIM = 128
HGRN_WIDTH = HGRN_HEADS * HGRN_HEAD_DIM
MIX_WIDTH = ATTN_WIDTH + HGRN_WIDTH
DILATIONS = (1, 4, 16)
SPAN = 128
ROPE_THETA = 10000.0
MLP_HIDDEN = 4 * D_MODEL
NORM_EPS = 1e-6
MASK_VALUE = -1e30

LANES = 128
V7X_VMEM_LIMIT = 56 * 1024 * 1024
ROW_TILE = 512
PROJ_ROW_TILE = 1024
COL_CHUNK = 512
CHUNK = 128
ATTN_UNROLL = 4
PROJ_PIECE = 256
LOG2_E = 1.4426950408889634


def _rms(x, gain):
    return x * lax.rsqrt(jnp.mean(x * x, axis=-1, keepdims=True) + NORM_EPS) * gain


def _dot_nt(a, b):
    return lax.dot_general(a, b, (((1,), (1,)), ((), ())), preferred_element_type=F32)


def _attn_kernel(q_ref, k_ref, v_ref, o_ref, out_a, out_b, lse_a, lse_b, p_buf, m_buf,
                 *, seq):
    lane = lax.broadcasted_iota(jnp.int32, (SPAN, LANES), 1)
    head0 = lane < ATTN_HEAD_DIM
    q_minus_k = (lax.broadcasted_iota(jnp.int32, (SPAN, SPAN), 0)
                 - lax.broadcasted_iota(jnp.int32, (SPAN, SPAN), 1))
    prev_ok = q_minus_k <= 0
    cur_ok = q_minus_k >= 0
    ones = jnp.ones((SPAN, LANES), BF16)
    n_tiles = seq // SPAN

    assert DILATIONS[0] == 1 and len(DILATIONS) == 3
    parked = {DILATIONS[1]: (out_a, lse_a), DILATIONS[2]: (out_b, lse_b)}

    for dil in sorted(DILATIONS, reverse=True):
        n_blk = n_tiles // dil
        tiles = [(r, n) for r in range(dil) for n in range(n_blk)]

        def block_rows(r, n, dil=dil):
            start = r + dil * SPAN * n
            return pl.ds(start, SPAN) if dil == 1 else pl.ds(start, SPAN, stride=dil)

        def probs(group, block_rows=block_rows):
            key_blocks = {}

            def keys(r, n):
                if (r, n) not in key_blocks:
                    key_blocks[r, n] = k_ref[block_rows(r, n), :].astype(BF16)
                return key_blocks[r, n]

            for slot, (r, n) in enumerate(group):
                qt = q_ref[block_rows(r, n), :]
                kt = keys(r, n) if n == 0 else jnp.concatenate([keys(r, n - 1), keys(r, n)], 0)
                scores = []
                for head in range(2):
                    qh = jnp.where(head0 if head == 0 else ~head0, qt, 0.0).astype(BF16)
                    scores.append(_dot_nt(qh, kt))
                mx = []
                for head, s in enumerate(scores):
                    s_cur = jnp.where(cur_ok, s[:, -SPAN:], MASK_VALUE)
                    if n == 0:
                        m = jnp.max(s_cur, axis=-1, keepdims=True)
                    else:
                        s_prev = jnp.where(prev_ok, s[:, :SPAN], MASK_VALUE)
                        m = jnp.max(jnp.maximum(s_prev, s_cur), axis=-1, keepdims=True)
                        p_buf[slot, head, :, 0:SPAN] = jnp.exp2(s_prev - m).astype(BF16)
                    p_buf[slot, head, :, SPAN:2 * SPAN] = jnp.exp2(s_cur - m).astype(BF16)
                    mx.append(jnp.broadcast_to(m, (SPAN, LANES)))
                m_buf[slot] = jnp.where(head0, mx[0], mx[1])

        def weighted_values(group, block_rows=block_rows, dil=dil):
            value_blocks = {}

            def values(r, n):
                if (r, n) not in value_blocks:
                    value_blocks[r, n] = jnp.concatenate(
                        [v_ref[block_rows(r, n), :].astype(BF16), ones], axis=1)
                return value_blocks[r, n]

            for slot, (r, n) in enumerate(group):
                if n == 0:
                    vt = values(r, n)
                    res = [jnp.dot(p_buf[slot, head, :, SPAN:2 * SPAN], vt,
                                   preferred_element_type=F32) for head in range(2)]
                else:
                    vt = jnp.concatenate([values(r, n - 1), values(r, n)], axis=0)
                    res = [jnp.dot(p_buf[slot, head], vt, preferred_element_type=F32)
                           for head in range(2)]
                denom = jnp.where(head0, res[0][:, LANES:], res[1][:, LANES:])
                out = jnp.where(head0, res[0][:, :LANES], res[1][:, :LANES]) / denom
                lse = m_buf[slot] + jnp.log(denom) * LOG2_E
                q_rows = block_rows(r, n)
                if dil in parked:
                    parked[dil][0][q_rows, :] = out
                    parked[dil][1][q_rows, :] = lse
                else:
                    a, b = lse_a[q_rows, :], lse_b[q_rows, :]
                    top = jnp.maximum(jnp.maximum(lse, a), b)
                    w, wa, wb = jnp.exp2(lse - top), jnp.exp2(a - top), jnp.exp2(b - top)
                    o_ref[q_rows, :] = ((w * out + wa * out_a[q_rows, :] + wb * out_b[q_rows, :])
                                        / (w + wa + wb))

        groups = [tiles[i:i + ATTN_UNROLL] for i in range(0, len(tiles), ATTN_UNROLL)]
        probs(groups[0])
        for done, ahead in zip(groups[:-1], groups[1:]):
            weighted_values(done)
            probs(ahead)
        weighted_values(groups[-1])


def _dilated_attention(qkv3d):
    b, seq, _ = qkv3d.shape
    n_pairs = ATTN_WIDTH // LANES
    blk = lambda off: pl.BlockSpec((None, seq, LANES), lambda i, j: (i, 0, off + j))
    slab = pltpu.VMEM((seq, LANES), F32)
    return pl.pallas_call(
        functools.partial(_attn_kernel, seq=seq),
        grid=(b, n_pairs),
        in_specs=[blk(0), blk(n_pairs), blk(2 * n_pairs)],
        out_specs=pl.BlockSpec((None, seq, LANES), lambda i, j: (i, 0, j)),
        out_shape=jax.ShapeDtypeStruct((b, seq, ATTN_WIDTH), F32),
        scratch_shapes=[slab] * 4 + [
            pltpu.VMEM((ATTN_UNROLL, 2, SPAN, 2 * SPAN), BF16),
            pltpu.VMEM((ATTN_UNROLL, SPAN, LANES), F32)],
        compiler_params=pltpu.CompilerParams(
            dimension_semantics=("arbitrary", "arbitrary"),
            vmem_limit_bytes=V7X_VMEM_LIMIT),
        name="dilated_attn",
    )(qkv3d, qkv3d, qkv3d)


def _level_anchor(g, half):
    rows = g.shape[0]
    if 2 * half >= 8:
        g3 = g.reshape(rows // (2 * half), 2 * half, LANES)
        return jnp.broadcast_to(g3[:, half - 1:half, :], g3.shape).reshape(rows, LANES)
    g3 = g.reshape(rows // 8, 8, LANES)
    sub = lax.broadcasted_iota(jnp.int32, g3.shape, 1)
    out = jnp.broadcast_to(g3[:, half - 1:half, :], g3.shape)
    for first in range(2 * half, 8, 2 * half):
        src = jnp.broadcast_to(g3[:, first + half - 1:first + half, :], g3.shape)
        out = jnp.where(sub >= first, src, out)
    return out.reshape(rows, LANES)


def _neg_abs(x):
    return -jnp.abs(x)


def _chunk_cumsum(x):
    row = lax.broadcasted_iota(jnp.int32, x.shape, 0)
    shift = 1
    while shift < x.shape[0]:
        x = x + jnp.where(row >= shift, pltpu.roll(x, shift, axis=0), 0.0)
        shift *= 2
    return x


def _in_proj_hgrn_kernel(x_ref, cq_ref, sq_ref, ck_ref, sk_ref, nm_ref, w_ref, lbl_ref,
                         gain_ref, lvl_ref, qkv_ref, rec_ref, hin, state,
                         *, layer, tiles_per_seq):
    h = _rms(x_ref[...], nm_ref[...]).astype(BF16)
    head_cols = 4 * HGRN_HEAD_DIM
    restart = pl.program_id(0) % tiles_per_seq == 0

    logits = lbl_ref[...]
    e = jnp.exp(logits - jnp.max(logits, axis=0, keepdims=True))
    share = e / jnp.sum(e, axis=0, keepdims=True)
    lb_all = jnp.sum(share[0:layer + 1, :], axis=0, keepdims=True) - share[0:1, :]

    def rec_cols(head, kind):
        c = 3 * ATTN_WIDTH + kind * HGRN_WIDTH + head * HGRN_HEAD_DIM
        return slice(c, c + HGRN_HEAD_DIM)

    blocks = PROJ_PIECE // LANES
    pieces = []
    for head in range(HGRN_HEADS):
        for kind in range(0, 4, blocks):
            pieces.append((hin, head * head_cols + kind * LANES,
                           [rec_cols(head, kind + i) for i in range(blocks)], None))
    for c in range(0, 3 * ATTN_WIDTH, PROJ_PIECE):
        tables = ((cq_ref, sq_ref), (ck_ref, sk_ref), None)[c // ATTN_WIDTH]
        pieces.append((qkv_ref, c, [slice(c, c + PROJ_PIECE)], tables))

    lane = lax.broadcasted_iota(jnp.int32, (x_ref.shape[0], LANES), 1)
    first_half = (lane & (ATTN_HEAD_DIM // 2)) == 0

    def project(n_pieces):
        for _ in range(n_pieces):
            dst, c_dst, w_cols, tables = pieces.pop(0)
            w = jnp.concatenate([w_ref[:, cols] for cols in w_cols], axis=1)
            y = jnp.dot(h, w, preferred_element_type=F32)
            if tables is None:
                dst[:, c_dst:c_dst + PROJ_PIECE] = y
                continue
            cos, sin = tables[0][...], tables[1][...]
            for b0 in range(0, PROJ_PIECE, LANES):
                yb = y[:, b0:b0 + LANES]
                partner = jnp.where(first_half,
                                    pltpu.roll(yb, LANES - ATTN_HEAD_DIM // 2, axis=1),
                                    pltpu.roll(yb, ATTN_HEAD_DIM // 2, axis=1))
                dst[:, c_dst + b0:c_dst + b0 + LANES] = yb * cos + partner * sin

    project(head_cols // PROJ_PIECE)
    n_spread = len(pieces)
    chunks_per_head = x_ref.shape[0] // CHUNK
    n_slots = HGRN_HEADS * chunks_per_head
    level = lvl_ref[...]
    odd_row = (lax.broadcasted_iota(jnp.int32, (CHUNK, LANES), 0) & 1) == 1
    for head in range(HGRN_HEADS):
        c0 = head * head_cols
        lb = lb_all[:, head * HGRN_HEAD_DIM:(head + 1) * HGRN_HEAD_DIM]
        st = jnp.where(restart, 0.0, state[head])
        for c in range(chunks_per_head):
            r0 = c * CHUNK
            cols = lambda j: hin[r0:r0 + CHUNK, c0 + j * LANES:c0 + (j + 1) * LANES]
            out, st = _hgrn_chunk(cols(0), cols(1), cols(2), cols(3), st, lb,
                                  gain_ref[...], level, odd_row)
            rec_ref[r0:r0 + CHUNK, head * HGRN_HEAD_DIM:(head + 1) * HGRN_HEAD_DIM] = out
            slot = head * chunks_per_head + c
            due = -(-(slot + 1) * n_spread // n_slots)
            project(due - (n_spread - len(pieces)))
        state[head] = st
    assert not pieces


def _hgrn_chunk(qx, z, vx, gx, st, lb, gain, level, odd_row):
    n_levels = int(math.log2(CHUNK))
    q = qx * jax.nn.sigmoid(qx) * (HGRN_HEAD_DIM ** -0.5)
    v_t = vx.T.astype(BF16)
    t = jnp.exp2(_neg_abs(z) * LOG2_E)
    r = 1.0 / (1.0 + t)
    sig_pos = jnp.where(z >= 0, r, t * r)
    sig_neg = jnp.where(z >= 0, t * r, r)
    f = lb + (1.0 - lb) * sig_pos
    k = (1.0 - lb) * sig_neg
    g = _chunk_cumsum(jnp.log(f) * LOG2_E)
    g_last = g[CHUNK - 1:CHUNK, :]

    q_bf, k_bf = q.astype(BF16), k.astype(BF16)
    a = _dot_nt(q_bf, k_bf)
    a = jnp.where(level == n_levels, a, 0.0)
    for lv in range(n_levels):
        if lv == 0:
            x = jnp.where(odd_row, f, 1.0)
        else:
            x = jnp.exp2(_neg_abs(g - _level_anchor(g, 1 << lv)))
        x = x.astype(BF16)
        a = jnp.where(level == lv, _dot_nt(q_bf * x, k_bf * x), a)

    lhs = jnp.concatenate([a.astype(BF16), (q * jnp.exp2(g)).astype(BF16)], axis=1)
    rhs_t = jnp.concatenate([v_t, st.astype(BF16)], axis=1)
    o = _dot_nt(lhs, rhs_t)
    k_out = (k * jnp.exp2(g_last - g)).astype(BF16)
    st = st * jnp.exp2(g_last) + jnp.dot(v_t, k_out, preferred_element_type=F32)
    return _rms(o, gain) * (gx * jax.nn.sigmoid(gx)), st


def _in_proj_hgrn(x2d, rotary, norm_gain, w_bf16, lb_logits, out_gain, level_map, layer, seq):
    t, d = x2d.shape
    n = w_bf16.shape[1]
    tiles_per_seq = seq // PROJ_ROW_TILE
    full = lambda a: pl.BlockSpec(a.shape, lambda i: (0,) * a.ndim)
    row = lambda width: pl.BlockSpec((PROJ_ROW_TILE, width), lambda i: (i, 0))
    pos = pl.BlockSpec((PROJ_ROW_TILE, LANES), lambda i: (i % tiles_per_seq, 0))
    consts = (norm_gain.reshape(1, d), w_bf16, lb_logits, out_gain.reshape(1, LANES), level_map)
    return pl.pallas_call(
        functools.partial(_in_proj_hgrn_kernel, layer=layer, tiles_per_seq=tiles_per_seq),
        grid=(t // PROJ_ROW_TILE,),
        in_specs=[row(d)] + [pos] * len(rotary) + [full(a) for a in consts],
        out_specs=[row(3 * ATTN_WIDTH), row(HGRN_WIDTH)],
        out_shape=[jax.ShapeDtypeStruct((t, 3 * ATTN_WIDTH), F32),
                   jax.ShapeDtypeStruct((t, HGRN_WIDTH), F32)],
        scratch_shapes=[pltpu.VMEM((PROJ_ROW_TILE, n - 3 * ATTN_WIDTH), F32),
                        pltpu.VMEM((HGRN_HEADS, HGRN_HEAD_DIM, HGRN_HEAD_DIM), F32)],
        compiler_params=pltpu.CompilerParams(
            dimension_semantics=("arbitrary",), vmem_limit_bytes=V7X_VMEM_LIMIT),
        name="in_proj_hgrn",
    )(x2d, *rotary, *consts)


def _intra_chunk_levels():
    i = np.arange(CHUNK, dtype=np.int32)[:, None]
    j = np.arange(CHUNK, dtype=np.int32)[None, :]
    x = i ^ j
    lv = np.zeros((CHUNK, CHUNK), np.int32)
    for bit in range(1, int(math.log2(CHUNK))):
        lv = np.where(x >= (1 << bit), bit, lv)
    lv = np.where(i == j, int(math.log2(CHUNK)), lv)
    return jnp.asarray(np.where(j > i, -1, lv).astype(np.int32))


def _mix_mlp_kernel(x_ref, attn_ref, rec_ref, ag_ref, wo_ref, nm_ref, wu_ref, wd_ref,
                    nf_ref, o_ref, *, final_norm):
    attn = _rms(attn_ref[...], ag_ref[...]).astype(BF16)
    x1 = (x_ref[...]
          + jnp.dot(attn, wo_ref[0:ATTN_WIDTH, :], preferred_element_type=F32)
          + jnp.dot(rec_ref[...].astype(BF16), wo_ref[ATTN_WIDTH:MIX_WIDTH, :],
                    preferred_element_type=F32))
    h = _rms(x1, nm_ref[...]).astype(BF16)
    o_ref[...] = x1
    for c0 in range(0, MLP_HIDDEN, COL_CHUNK):
        u = jnp.dot(h, wu_ref[:, c0:c0 + COL_CHUNK], preferred_element_type=F32)
        u = jnp.square(jnp.maximum(u, 0.0)).astype(BF16)
        o_ref[...] += jnp.dot(u, wd_ref[c0:c0 + COL_CHUNK, :], preferred_element_type=F32)
    if final_norm:
        o_ref[...] = _rms(o_ref[...], nf_ref[...])


def _mix_mlp(x2d, attn2d, rec2d, attn_gain, w_out, norm_mlp, w_up, w_down, norm_final,
             final_norm):
    t, d = x2d.shape
    row = lambda w: pl.BlockSpec((ROW_TILE, w), lambda i: (i, 0))
    full = lambda a: pl.BlockSpec(a.shape, lambda i: (0, 0))
    vec = lambda a: a.reshape(1, -1)
    args = (x2d, attn2d, rec2d, vec(attn_gain), w_out, vec(norm_mlp), w_up, w_down,
            vec(norm_final))
    return pl.pallas_call(
        functools.partial(_mix_mlp_kernel, final_norm=final_norm),
        grid=(t // ROW_TILE,),
        in_specs=[row(d), row(ATTN_WIDTH), row(HGRN_WIDTH)] + [full(a) for a in args[3:]],
        out_specs=row(d),
        out_shape=jax.ShapeDtypeStruct((t, d), F32),
        compiler_params=pltpu.CompilerParams(
            dimension_semantics=("arbitrary",), vmem_limit_bytes=V7X_VMEM_LIMIT),
        name="mix_mlp",
    )(*args)


def _rotary_tables(seq):
    half = ATTN_HEAD_DIM // 2
    inv_freq = ROPE_THETA ** (-np.arange(half, dtype=np.float64) / half)
    ang = np.arange(seq, dtype=np.float64)[:, None] * inv_freq[None, :]
    cos, sin = np.cos(ang), np.sin(ang)
    cos = np.concatenate([cos, cos, cos, cos], axis=1)
    sin = np.concatenate([-sin, sin, -sin, sin], axis=1)
    q_scale = ATTN_HEAD_DIM ** -0.5 * LOG2_E
    return tuple(jnp.asarray(t.astype(np.float32))
                 for t in (cos * q_scale, sin * q_scale, cos, sin))


def kernel(x, norm_mix, w_in, attn_out_gain, hgrn_lb_logits, hgrn_out_gain, w_out, norm_mlp,
           w_up, w_down, norm_final):
    b, seq, d = x.shape
    depth = w_in.shape[0]
    rotary = _rotary_tables(seq)
    level_map = _intra_chunk_levels()
    x2d = x.reshape(b * seq, d)
    w_in_bf16 = w_in.astype(BF16)
    for layer in range(depth):
        qkv, rec = _in_proj_hgrn(x2d, rotary, norm_mix[layer], w_in_bf16[layer],
                                 hgrn_lb_logits, hgrn_out_gain[layer], level_map, layer, seq)
        attn = _dilated_attention(qkv.reshape(b, seq, 3 * ATTN_WIDTH))
        x2d = _mix_mlp(x2d, attn.reshape(b * seq, ATTN_WIDTH), rec,
                       attn_out_gain[layer], w_out[layer].astype(BF16), norm_mlp[layer],
                       w_up[layer].astype(BF16), w_down[layer].astype(BF16), norm_final,
                       final_norm=(layer == depth - 1))
    return x2d.reshape(b, seq, d)
```
